```python
import math
import jax, jax.numpy as jnp
from jax import lax
import numpy as np

D_MODEL = 1024
BATCH = 8
SEQ = 4096
DEPTH = 2

RMS_EPS = 1e-6
ROPE_THETA = 10000.0
ATTN_Q_BLOCK = 128

MLA_HEADS = 4
MLA_NOPE_DIM = 128
MLA_ROPE_DIM = 64
MLA_V_DIM = 128
MLA_QK_DIM = MLA_NOPE_DIM + MLA_ROPE_DIM
MLA_Q_RANK = 384
MLA_KV_RANK = 256
MLA_WIDTH = MLA_HEADS * MLA_V_DIM

DN_HEADS = 4
DN_HEAD_DIM = 128
DN_WIDTH = DN_HEADS * DN_HEAD_DIM
DN_CONV = 4
DN_CHUNK = 64

DIL_WINDOWS = (128, 512, 2048)
DIL_DILATIONS = (1, 4, 16)
DIL_GROUPS = 3
DIL_HEADS_PER_GROUP = 4
DIL_HEAD_DIM = 128
DIL_QKV_WIDTH = DIL_GROUPS * DIL_HEADS_PER_GROUP * DIL_HEAD_DIM
DIL_WIDTH = DIL_HEADS_PER_GROUP * DIL_HEAD_DIM
DIL_BLOCK = 128

N_BRANCHES = 3
BRANCH_WIDTH = 512
IN_SPLITS = (MLA_Q_RANK, MLA_KV_RANK + MLA_ROPE_DIM, MLA_WIDTH,
             3 * DN_WIDTH, DN_HEADS, DN_HEADS, DN_WIDTH,
             3 * DIL_QKV_WIDTH, DIL_WIDTH,
             N_BRANCHES * D_MODEL)
IN_WIDTH = (MLA_Q_RANK + MLA_KV_RANK + MLA_ROPE_DIM + MLA_WIDTH
            + 3 * DN_WIDTH + 2 * DN_HEADS + DN_WIDTH
            + 3 * DIL_QKV_WIDTH + DIL_WIDTH + N_BRANCHES * D_MODEL)

kernel_name = "hybrid_mla_gdn_dilated_gated_merge"


def rms_norm(x, g):
    xf = x.astype(jnp.float32)
    y = xf * lax.rsqrt(jnp.mean(xf * xf, axis=-1, keepdims=True) + RMS_EPS)
    return (y * g.astype(jnp.float32)).astype(x.dtype)


def l2_normalize(x):
    return x * lax.rsqrt(jnp.sum(x * x, axis=-1, keepdims=True) + 1e-6)


def rope_tables(positions, dim):
    inv_freq = 1.0 / (ROPE_THETA ** (jnp.arange(0, dim, 2, dtype=jnp.float32) / dim))
    ang = positions.astype(jnp.float32)[..., None] * inv_freq
    return jnp.cos(ang), jnp.sin(ang)


def apply_rope(x, cos, sin):
    half = x.shape[-1] // 2
    xf = x.astype(jnp.float32)
    x1, x2 = xf[..., :half], xf[..., half:]
    c, s = cos[:, :, None, :], sin[:, :, None, :]
    return jnp.concatenate([x1 * c - x2 * s, x2 * c + x1 * s], axis=-1).astype(x.dtype)


def split_last(t, sizes):
    cuts, acc = [], 0
    for size in sizes[:-1]:
        acc += size
        cuts.append(acc)
    return jnp.split(t, cuts, axis=-1)


def causal_depthwise_conv(x, w):
    k = w.shape[0]
    return lax.conv_general_dilated(
        x, w[:, None, :].astype(x.dtype), window_strides=(1,), padding=[(k - 1, 0)],
        dimension_numbers=('NWC', 'WIO', 'NWC'), feature_group_count=x.shape[-1])


def causal_block_attention(q, k, v, scale):
    B, S, H, dk = q.shape
    dv = v.shape[-1]
    nb = S // ATTN_Q_BLOCK
    q_blocks = jnp.moveaxis(q.reshape(B, nb, ATTN_Q_BLOCK, H, dk), 1, 0)
    k_pos = jnp.arange(S)

    def one_block(args):
        q_blk, blk = args
        s = jnp.einsum('bqhd,bkhd->bhqk', q_blk, k, preferred_element_type=jnp.float32) * scale
        q_pos = blk * ATTN_Q_BLOCK + jnp.arange(ATTN_Q_BLOCK)
        s = jnp.where(k_pos[None, :] <= q_pos[:, None], s, -jnp.inf)
        p = jax.nn.softmax(s, axis=-1)
        return jnp.einsum('bhqk,bkhd->bqhd', p.astype(v.dtype), v)

    o = lax.map(one_block, (q_blocks, jnp.arange(nb)))
    return jnp.moveaxis(o, 0, 1).reshape(B, S, H, dv)


def mla_branch(q_lat, kv_lat, cos, sin, q_a_norm_g, w_q_b, kv_a_norm_g, w_kv_b, q_norm_g, k_norm_g):
    B, S, _ = q_lat.shape
    q = jnp.einsum('bsr,re->bse', rms_norm(q_lat, q_a_norm_g), w_q_b).reshape(B, S, MLA_HEADS, MLA_QK_DIM)
    c_kv, k_pe = kv_lat[..., :MLA_KV_RANK], kv_lat[..., MLA_KV_RANK:]
    kv = jnp.einsum('bsr,re->bse', rms_norm(c_kv, kv_a_norm_g), w_kv_b).reshape(B, S, MLA_HEADS, MLA_NOPE_DIM + MLA_V_DIM)
    k_nope, v = kv[..., :MLA_NOPE_DIM], kv[..., MLA_NOPE_DIM:]
    q_nope = rms_norm(q[..., :MLA_NOPE_DIM], q_norm_g[:MLA_NOPE_DIM])
    q_pe = apply_rope(rms_norm(q[..., MLA_NOPE_DIM:], q_norm_g[MLA_NOPE_DIM:]), cos, sin)
    k_nope = rms_norm(k_nope, k_norm_g[:MLA_NOPE_DIM])
    k_pe = apply_rope(rms_norm(k_pe, k_norm_g[MLA_NOPE_DIM:])[:, :, None, :], cos, sin)
    q_full = jnp.concatenate([q_nope, q_pe], axis=-1)
    k_full = jnp.concatenate([k_nope, jnp.broadcast_to(k_pe, (B, S, MLA_HEADS, MLA_ROPE_DIM))], axis=-1)
    o = causal_block_attention(q_full, k_full, v, MLA_QK_DIM ** -0.5)
    return o.reshape(B, S, MLA_WIDTH)


def chunk_gated_delta_rule(q, k, v, g, beta):
    B, S, H, dk = q.shape
    dv = v.shape[-1]
    C = DN_CHUNK
    N = S // C

    def chunks(t):
        t = jnp.moveaxis(t, 2, 1)
        return t.reshape(B, H, N, C, *t.shape[3:])

    q, k, v, g, beta = (chunks(t) for t in (q, k, v, g, beta))
    gc = jnp.cumsum(g, axis=-1)
    idx = jnp.arange(C)
    incl = idx[:, None] >= idx[None, :]
    strict = idx[:, None] > idx[None, :]
    decay = jnp.exp(jnp.where(incl, gc[..., :, None] - gc[..., None, :], -jnp.inf))
    k_beta = k * beta[..., None]
    lower = jnp.where(strict, jnp.einsum('bhncd,bhnmd->bhncm', k_beta, k) * decay, 0.0)
    rhs = jnp.concatenate([v * beta[..., None], k_beta * jnp.exp(gc)[..., None]], axis=-1)
    sol = lax.linalg.triangular_solve(lower + jnp.eye(C, dtype=lower.dtype), rhs, left_side=True, lower=True)
    u, w = sol[..., :dv], sol[..., dv:]
    qk = jnp.einsum('bhncd,bhnmd->bhncm', q, k) * decay

    def step(state, inp):
        q_i, k_i, u_i, w_i, gc_i, qk_i = inp
        v_new = u_i - jnp.einsum('bhcd,bhde->bhce', w_i, state)
        o_i = (jnp.einsum('bhcd,bhde->bhce', q_i * jnp.exp(gc_i)[..., None], state)
               + jnp.einsum('bhcm,bhme->bhce', qk_i, v_new))
        g_last = gc_i[..., -1:]
        state = (state * jnp.exp(g_last)[..., None]
                 + jnp.einsum('bhcd,bhce->bhde', k_i * jnp.exp(g_last - gc_i)[..., None], v_new))
        return state, o_i

    xs = tuple(jnp.moveaxis(t, 2, 0) for t in (q, k, u, w, gc, qk))
    state0 = jnp.zeros((B, H, dk, dv), jnp.float32)
    _, o = lax.scan(step, state0, xs)
    o = jnp.moveaxis(o, 0, 2).reshape(B, H, S, dv)
    return jnp.moveaxis(o, 1, 2)


def gated_deltanet_branch(qkv, a, b, conv_w, a_log, dt_bias, out_norm_g):
    B, S, _ = qkv.shape
    mixed = jax.nn.silu(causal_depthwise_conv(qkv, conv_w)).astype(jnp.float32)
    q, k, v = jnp.split(mixed, 3, axis=-1)
    q = l2_normalize(q.reshape(B, S, DN_HEADS, DN_HEAD_DIM)) * (DN_HEAD_DIM ** -0.5)
    k = l2_normalize(k.reshape(B, S, DN_HEADS, DN_HEAD_DIM))
    v = v.reshape(B, S, DN_HEADS, DN_HEAD_DIM)
    beta = jax.nn.sigmoid(b.astype(jnp.float32))
    g = -jnp.exp(a_log.astype(jnp.float32)) * jax.nn.softplus(a.astype(jnp.float32) + dt_bias.astype(jnp.float32))
    o = chunk_gated_delta_rule(q, k, v, g, beta)
    return rms_norm(o, out_norm_g).astype(qkv.dtype).reshape(B, S, DN_WIDTH)


def dilated_window_attention(q, k, v, window, dilation):
    B, S, H, hd = q.shape
    reach = window // dilation
    L = S // dilation
    nb = -(-L // DIL_BLOCK)
    Lp = nb * DIL_BLOCK

    def to_blocks(t):
        t = jnp.swapaxes(t.reshape(B, L, dilation, H, hd), 1, 2)
        t = jnp.pad(t, ((0, 0), (0, 0), (0, Lp - L), (0, 0), (0, 0)))
        return t.reshape(B, dilation, nb, DIL_BLOCK, H, hd)

    def with_previous(t):
        prev = jnp.pad(t[:, :, :-1], ((0, 0), (0, 0), (1, 0), (0, 0), (0, 0), (0, 0)))
        return jnp.concatenate([prev, t], axis=3)

    qb = to_blocks(q)
    kb = with_previous(to_blocks(k))
    vb = with_previous(to_blocks(v))
    s = jnp.einsum('bgnqhd,bgnkhd->bgnhqk', qb, kb, preferred_element_type=jnp.float32) * (hd ** -0.5)
    qi = jnp.arange(DIL_BLOCK)[:, None]
    kc = jnp.arange(2 * DIL_BLOCK)[None, :]
    dist = DIL_BLOCK + qi - kc
    band = (dist >= 0) & (dist <= reach)
    has_prev = (jnp.arange(nb) > 0)[:, None, None] | (kc >= DIL_BLOCK)[None]
    valid = band[None] & has_prev
    s = jnp.where(valid[:, None], s, -jnp.inf)
    m = jnp.max(s, axis=-1, keepdims=True)
    e = jnp.exp(s - m)
    den = jnp.sum(e, axis=-1)
    o = jnp.einsum('bgnhqk,bgnkhd->bgnqhd', e, vb.astype(jnp.float32)) / jnp.swapaxes(den, -1, -2)[..., None]
    lse = jnp.swapaxes(m[..., 0] + jnp.log(den), -1, -2)

    def from_blocks(t):
        t = t.reshape(B, dilation, Lp, *t.shape[4:])[:, :, :L]
        return jnp.swapaxes(t, 1, 2).reshape(B, S, *t.shape[3:])

    return from_blocks(o), from_blocks(lse)


def dilated_branch(qkv, cos, sin, q_norm_g, k_norm_g):
    B, S, _ = qkv.shape
    n_heads = DIL_GROUPS * DIL_HEADS_PER_GROUP
    q, k, v = (t.reshape(B, S, n_heads, DIL_HEAD_DIM) for t in jnp.split(qkv, 3, axis=-1))
    q = apply_rope(rms_norm(q, q_norm_g), cos, sin)
    k = apply_rope(rms_norm(k, k_norm_g), cos, sin)
    grp = (B, S, DIL_GROUPS, DIL_HEADS_PER_GROUP, DIL_HEAD_DIM)
    q, k, v = q.reshape(grp), k.reshape(grp), v.reshape(grp)
    outs, lses = [], []
    for gi in range(DIL_GROUPS):
        o_g, lse_g = dilated_window_attention(q[:, :, gi], k[:, :, gi], v[:, :, gi], DIL_WINDOWS[gi], DIL_DILATIONS[gi])
        outs.append(o_g)
        lses.append(lse_g)
    wts = jax.nn.softmax(jnp.stack(lses, axis=0), axis=0)
    o = jnp.sum(wts[..., None] * jnp.stack(outs, axis=0), axis=0)
    return o.reshape(B, S, DIL_WIDTH).astype(qkv.dtype)


def hybrid_layer(x, cos_r, sin_r, cos_h, sin_h, norm_g, w_in, mla_q_a_norm_g, mla_w_q_b,
                 mla_kv_a_norm_g, mla_w_kv_b, mla_q_norm_g, mla_k_norm_g, dn_conv_w, dn_a_log,
                 dn_dt_bias, dn_out_norm_g, dil_q_norm_g, dil_k_norm_g, w_branch, w_out):
    B, S, _ = x.shape
    h = rms_norm(x, norm_g)
    proj = jnp.einsum('bsd,de->bse', h, w_in)
    (q_lat, kv_lat, z_a, dn_qkv, dn_a, dn_b, z_b, dil_qkv, z_c, gate_logits) = split_last(proj, IN_SPLITS)
    y_a = mla_branch(q_lat, kv_lat, cos_r, sin_r, mla_q_a_norm_g, mla_w_q_b, mla_kv_a_norm_g,
                     mla_w_kv_b, mla_q_norm_g, mla_k_norm_g)
    y_b = gated_deltanet_branch(dn_qkv, dn_a, dn_b, dn_conv_w, dn_a_log, dn_dt_bias, dn_out_norm_g)
    y_c = dilated_branch(dil_qkv, cos_h, sin_h, dil_q_norm_g, dil_k_norm_g)
    ys = jnp.stack([y_a * jax.nn.silu(z_a), y_b * jax.nn.silu(z_b), y_c * jax.nn.silu(z_c)], axis=2)
    branch_out = jnp.einsum('bsnc,ncd->bsnd', ys, w_branch)
    gates = jax.nn.sigmoid(gate_logits.reshape(B, S, N_BRANCHES, D_MODEL))
    mixed = jnp.sum(gates * branch_out, axis=2)
    return x + jnp.einsum('bsd,de->bse', mixed, w_out)


def setup_inputs(seed: int = 0) -> dict:
    key = jax.random.key(seed)
    ks = jax.random.split(key, 24)
    f32 = jnp.float32

    def normal(k, shape, fan_in):
        return jax.random.normal(k, shape, f32) * (fan_in ** -0.5)

    def gain(k, shape):
        return 1.0 + 0.02 * jax.random.normal(k, shape, f32)

    x = jax.random.normal(ks[0], (BATCH, SEQ, D_MODEL), f32)
    offsets = jax.random.randint(ks[1], (BATCH, 1), 0, 1024, dtype=jnp.int32)
    positions = jnp.arange(SEQ, dtype=jnp.int32)[None, :] + offsets
    dt = jnp.exp(jax.random.uniform(ks[14], (DEPTH, DN_HEADS), f32, math.log(1e-3), math.log(1e-1)))
    return {
        "x": x,
        "positions": positions,
        "norm_g": gain(ks[2], (DEPTH, D_MODEL)),
        "w_in": normal(ks[3], (DEPTH, D_MODEL, IN_WIDTH), D_MODEL),
        "mla_q_a_norm_g": gain(ks[4], (DEPTH, MLA_Q_RANK)),
        "mla_w_q_b": normal(ks[5], (DEPTH, MLA_Q_RANK, MLA_HEADS * MLA_QK_DIM), MLA_Q_RANK),
        "mla_kv_a_norm_g": gain(ks[6], (DEPTH, MLA_KV_RANK)),
        "mla_w_kv_b": normal(ks[7], (DEPTH, MLA_KV_RANK, MLA_HEADS * (MLA_NOPE_DIM + MLA_V_DIM)), MLA_KV_RANK),
        "mla_q_norm_g": gain(ks[8], (DEPTH, MLA_QK_DIM)),
        "mla_k_norm_g": gain(ks[9], (DEPTH, MLA_QK_DIM)),
        "dn_conv_w": normal(ks[10], (DEPTH, DN_CONV, 3 * DN_WIDTH), DN_CONV),
        "dn_a_log": jnp.log(jax.random.uniform(ks[11], (DEPTH, DN_HEADS), f32, 1.0, 16.0)),
        "dn_dt_bias": dt + jnp.log(-jnp.expm1(-dt)),
        "dn_out_norm_g": gain(ks[12], (DEPTH, DN_HEAD_DIM)),
        "dil_q_norm_g": gain(ks[13], (DEPTH, DIL_HEAD_DIM)),
        "dil_k_norm_g": gain(ks[15], (DEPTH, DIL_HEAD_DIM)),
        "w_branch": normal(ks[16], (DEPTH, N_BRANCHES, BRANCH_WIDTH, D_MODEL), BRANCH_WIDTH),
        "w_out": normal(ks[17], (DEPTH, D_MODEL, D_MODEL), D_MODEL),
    }


def reference(x, positions, norm_g, w_in, mla_q_a_norm_g, mla_w_q_b, mla_kv_a_norm_g, mla_w_kv_b,
              mla_q_norm_g, mla_k_norm_g, dn_conv_w, dn_a_log, dn_dt_bias, dn_out_norm_g,
              dil_q_norm_g, dil_k_norm_g, w_branch, w_out):
    cos_r, sin_r = rope_tables(positions, MLA_ROPE_DIM)
    cos_h, sin_h = rope_tables(positions, DIL_HEAD_DIM)
    for layer in range(DEPTH):
        x = hybrid_layer(x, cos_r, sin_r, cos_h, sin_h, norm_g[layer], w_in[layer],
                         mla_q_a_norm_g[layer], mla_w_q_b[layer], mla_kv_a_norm_g[layer],
                         mla_w_kv_b[layer], mla_q_norm_g[layer], mla_k_norm_g[layer],
                         dn_conv_w[layer], dn_a_log[layer], dn_dt_bias[layer], dn_out_norm_g[layer],
                         dil_q_norm_g[layer], dil_k_norm_g[layer], w_branch[layer], w_out[layer])
    return x
```

```python
import functools

import jax
import jax.numpy as jnp
from jax import lax
from jax.experimental import pallas as pl
from jax.experimental.pallas import tpu as pltpu

F32 = jnp.float32
BF16 = jnp.bfloat16

RMS_EPS = 1e-6
ROPE_THETA = 10000.0
LANE = 128

MLA_HEADS = 4
MLA_NOPE = 128
MLA_ROPE = 64
MLA_V = 128
MLA_QK = MLA_NOPE + MLA_ROPE
MLA_Q_RANK = 384
MLA_KV_RANK = 256
MLA_KV_PAD = 384
MLA_HEAD_PAD = 256

DN_HEADS = 4
DN_HD = 128
DN_WIDTH = DN_HEADS * DN_HD
DN_CONV = 4
DN_CHUNK = 64
DN_HALO = 16

DIL_WINDOWS = (128, 512, 2048)
DIL_DILATIONS = (1, 4, 16)
DIL_GROUPS = 3
DIL_HPG = 4
DIL_HD = 128
DIL_BLOCK = 128
DIL_QKV_WIDTH = DIL_GROUPS * DIL_HPG * DIL_HD

N_BRANCHES = 3
BRANCH_WIDTH = 512

W_QLAT = MLA_Q_RANK
W_KVLAT = MLA_KV_PAD
W_DNQKV = 3 * DN_WIDTH
W_DNAB = LANE
W_DIL = 3 * DIL_QKV_WIDTH
W_Z = N_BRANCHES * BRANCH_WIDTH

VMEM_LIMIT = 56 * 1024 * 1024


def _cparams(*sem):
    return pltpu.CompilerParams(dimension_semantics=sem, vmem_limit_bytes=VMEM_LIMIT)


def _nt(a, b):
    return lax.dot_general(a, b, (((1,), (1,)), ((), ())), preferred_element_type=F32)


def _dot(a, b):
    return jnp.dot(a, b, preferred_element_type=F32)


def _dot_split3(a, x):
    hi = x.astype(BF16)
    r1 = x - hi.astype(F32)
    mid = r1.astype(BF16)
    lo = (r1 - mid.astype(F32)).astype(BF16)
    return _dot(a, hi) + _dot(a, mid) + _dot(a, lo)


def _rope_kernel(pos_ref, c_ref, cosh_ref, sinh_ref, cr_ref, sa_ref, sb_ref):
    pos = pos_ref[...].astype(F32)
    c = c_ref[...]
    ang_h = pos * c[0:1, :]
    cosh_ref[...] = jnp.cos(ang_h)
    sinh_ref[...] = jnp.sin(ang_h) * c[1:2, :]
    ang_r = pos * c[2:3, :]
    sr = jnp.sin(ang_r)
    cr_ref[...] = jnp.cos(ang_r) * c[3:4, :]
    sa_ref[...] = sr * c[4:5, :]
    sb_ref[...] = sr * c[5:6, :]


def _rope_tables(positions):
    t = positions.size
    ts = min(t, 1024)
    half_h = DIL_HD // 2
    half_r = MLA_ROPE // 2
    inv_h = 1.0 / (ROPE_THETA ** (jnp.arange(0, DIL_HD, 2, dtype=F32) / DIL_HD))
    inv_r = 1.0 / (ROPE_THETA ** (jnp.arange(0, MLA_ROPE, 2, dtype=F32) / MLA_ROPE))
    zeros = jnp.zeros
    ones = jnp.ones
    rows = [
        jnp.concatenate([inv_h, inv_h]),
        jnp.concatenate([-ones(half_h, F32), ones(half_h, F32)]),
        jnp.concatenate([inv_r, inv_r, zeros(LANE - MLA_ROPE, F32)]),
        jnp.concatenate([ones(MLA_ROPE, F32), zeros(LANE - MLA_ROPE, F32)]),
        jnp.concatenate([-ones(half_r, F32), zeros(LANE - half_r, F32)]),
        jnp.concatenate([zeros(half_r, F32), ones(half_r, F32), zeros(LANE - MLA_ROPE, F32)]),
        zeros(LANE, F32),
        zeros(LANE, F32),
    ]
    consts = jnp.stack(rows)
    tab = jax.ShapeDtypeStruct((t, LANE), F32)
    spec = pl.BlockSpec((ts, LANE), lambda i: (i, 0))
    return pl.pallas_call(
        _rope_kernel,
        grid=(t // ts,),
        in_specs=[pl.BlockSpec((ts, 1), lambda i: (i, 0)), pl.BlockSpec((8, LANE), lambda i: (0, 0))],
        out_specs=[spec] * 5,
        out_shape=[tab] * 5,
        compiler_params=_cparams("parallel"),
        name="rope_tables",
    )(positions.reshape(t, 1), consts)


_INPROJ_COLS = 512


def _inproj_kernel(x_ref, g_ref, w_ref, *out_refs):
    x = x_ref[...]
    h = (x * lax.rsqrt(jnp.mean(x * x, axis=-1, keepdims=True) + RMS_EPS) * g_ref[...]).astype(BF16)
    off = 0
    for o_ref in out_refs:
        width = o_ref.shape[-1]
        for c0 in range(0, width, _INPROJ_COLS):
            cw = min(_INPROJ_COLS, width - c0)
            acc = _dot(h, w_ref[:, off + c0:off + c0 + cw])
            o_ref[:, c0:c0 + cw] = acc.astype(o_ref.dtype)
        off += width


def _in_projection(x2, norm_g, w_in_p):
    t, d = x2.shape
    tm = min(t, 256)
    widths = (W_QLAT, W_KVLAT, W_DNQKV, W_DNAB, W_DIL, W_Z, N_BRANCHES * d)
    dtypes = (BF16, BF16, BF16, F32, BF16, BF16, BF16)
    assert sum(widths) == w_in_p.shape[1]
    return pl.pallas_call(
        _inproj_kernel,
        grid=(t // tm,),
        in_specs=[
            pl.BlockSpec((tm, d), lambda i: (i, 0)),
            pl.BlockSpec((1, d), lambda i: (0, 0)),
            pl.BlockSpec(memory_space=pltpu.VMEM),
        ],
        out_specs=[pl.BlockSpec((tm, w), lambda i: (i, 0)) for w in widths],
        out_shape=[jax.ShapeDtypeStruct((t, w), dt) for w, dt in zip(widths, dtypes)],
        compiler_params=_cparams("parallel"),
        name="in_projection",
    )(x2, norm_g.reshape(1, d), w_in_p)


def _mla_prep_kernel(qlat_ref, kvlat_ref, cr_ref, sa_ref, sb_ref, gqa_ref, wq_ref, gkva_ref, wkv_ref,
                     gq_ref, gk_ref, q_ref, k_ref, v_ref):
    def rms(x, gain, n):
        return x * lax.rsqrt(jnp.sum(x * x, axis=-1, keepdims=True) * (1.0 / n) + RMS_EPS) * gain

    cr, sa, sb = cr_ref[...], sa_ref[...], sb_ref[...]

    def rope(x):
        return x * cr + pltpu.roll(x, LANE - MLA_ROPE // 2, 1) * sa + pltpu.roll(x, MLA_ROPE // 2, 1) * sb

    scale = MLA_QK ** -0.5
    qn = rms(qlat_ref[...].astype(F32), gqa_ref[...], MLA_Q_RANK).astype(BF16)
    q = _dot(qn, wq_ref[...])
    kvl = kvlat_ref[...].astype(F32)
    cn = rms(kvl[:, :MLA_KV_RANK], gkva_ref[...], MLA_KV_RANK).astype(BF16)
    kv = _dot(cn, wkv_ref[...])
    k_pe = rope(rms(kvl[:, MLA_KV_RANK:], gk_ref[1:2, :], MLA_ROPE)).astype(BF16)
    for h in range(MLA_HEADS):
        c0 = MLA_HEAD_PAD * h
        q_nope = rms(q[:, c0:c0 + MLA_NOPE], gq_ref[0:1, :], MLA_NOPE)
        q_pe = rope(rms(q[:, c0 + MLA_NOPE:c0 + MLA_HEAD_PAD], gq_ref[1:2, :], MLA_ROPE))
        q_ref[:, c0:c0 + MLA_NOPE] = (q_nope * scale).astype(BF16)
        q_ref[:, c0 + MLA_NOPE:c0 + MLA_HEAD_PAD] = (q_pe * scale).astype(BF16)
        k_nope = rms(kv[:, MLA_NOPE * h:MLA_NOPE * (h + 1)], gk_ref[0:1, :], MLA_NOPE)
        k_ref[:, c0:c0 + MLA_NOPE] = k_nope.astype(BF16)
        k_ref[:, c0 + MLA_NOPE:c0 + MLA_HEAD_PAD] = k_pe
    v_ref[...] = kv[:, MLA_HEADS * MLA_NOPE:].astype(BF16)


def _mla_prep(qlat, kvlat, cr, sa, sb, gqa, wq, gkva, wkv, gq, gk):
    t = qlat.shape[0]
    tm = min(t, 512)
    row = lambda w: pl.BlockSpec((tm, w), lambda i: (i, 0))
    full = lambda a: pl.BlockSpec(a.shape, lambda i: (0,) * a.ndim)
    qk_w = MLA_HEADS * MLA_HEAD_PAD
    v_w = MLA_HEADS * MLA_V
    return pl.pallas_call(
        _mla_prep_kernel,
        grid=(t // tm,),
        in_specs=[row(W_QLAT), row(W_KVLAT), row(LANE), row(LANE), row(LANE),
                  full(gqa), full(wq), full(gkva), full(wkv), full(gq), full(gk)],
        out_specs=[row(qk_w), row(qk_w), row(v_w)],
        out_shape=[jax.ShapeDtypeStruct((t, qk_w), BF16), jax.ShapeDtypeStruct((t, qk_w), BF16),
                   jax.ShapeDtypeStruct((t, v_w), BF16)],
        compiler_params=_cparams("parallel"),
        name="mla_prep",
    )(qlat, kvlat, cr, sa, sb, gqa, wq, gkva, wkv, gq, gk)


def _mla_attn_kernel(q_ref, k_ref, v_ref, o_ref, m_ref, l_ref, acc_ref, *, tq):
    qi = pl.program_id(2)
    q = q_ref[0]
    m_ref[...] = jnp.full(m_ref.shape, -jnp.inf, F32)
    l_ref[...] = jnp.zeros(l_ref.shape, F32)
    acc_ref[...] = jnp.zeros(acc_ref.shape, F32)

    def step(j, diagonal):
        start = pl.multiple_of(j * tq, tq)
        kb = k_ref[0, pl.ds(start, tq), :]
        vb = v_ref[0, pl.ds(start, tq), :]
        s = _nt(q, kb)
        if diagonal:
            row = lax.broadcasted_iota(jnp.int32, s.shape, 0)
            col = lax.broadcasted_iota(jnp.int32, s.shape, 1)
            s = jnp.where(col <= row, s, -jnp.inf)
        m_old = m_ref[...]
        m_new = jnp.maximum(m_old, jnp.max(s, axis=-1, keepdims=True))
        alpha = jnp.exp(m_old - m_new)
        p = jnp.exp(s - m_new)
        l_ref[...] = alpha * l_ref[...] + jnp.sum(p, axis=-1, keepdims=True)
        acc_ref[...] = alpha * acc_ref[...] + _dot(p.astype(BF16), vb)
        m_ref[...] = m_new

    def body(j, carry):
        step(j, False)
        return carry

    lax.fori_loop(0, qi, body, 0)
    step(qi, True)
    o_ref[0] = (acc_ref[...] / l_ref[...]).astype(o_ref.dtype)


def _mla_attention(q, k, v):
    b, s, _ = q.shape
    tq = min(s, 512)
    return pl.pallas_call(
        functools.partial(_mla_attn_kernel, tq=tq),
        grid=(b, MLA_HEADS, s // tq),
        in_specs=[
            pl.BlockSpec((1, tq, MLA_HEAD_PAD), lambda bi, h, i: (bi, i, h)),
            pl.BlockSpec((1, s, MLA_HEAD_PAD), lambda bi, h, i: (bi, 0, h)),
            pl.BlockSpec((1, s, MLA_V), lambda bi, h, i: (bi, 0, h)),
        ],
        out_specs=pl.BlockSpec((1, tq, MLA_V), lambda bi, h, i: (bi, i, h)),
        out_shape=jax.ShapeDtypeStruct((b, s, MLA_HEADS * MLA_V), BF16),
        scratch_shapes=[pltpu.VMEM((tq, 1), F32), pltpu.VMEM((tq, 1), F32), pltpu.VMEM((tq, MLA_V), F32)],
        compiler_params=_cparams("parallel", "parallel", "arbitrary"),
        name="mla_attention",
    )(q, k, v)


def _block_diag(x, nblk, bw):
    blk = lax.broadcasted_iota(jnp.int32, x.shape, 1) // bw
    return jnp.concatenate([jnp.where(blk == h, x, 0.0) for h in range(nblk)], axis=0)


def _dn_kernel(x_ref, halo_ref, ab_ref, cw_ref, prm_ref, og_ref, y_ref,
               s_ref, xs_ref, q_s, k_s, v_s, g_s, b_s, *, tc):
    i = pl.program_id(1)
    nh, hd, ck = DN_HEADS, DN_HD, DN_CHUNK

    @pl.when(i == 0)
    def _():
        s_ref[...] = jnp.zeros(s_ref.shape, F32)

    xs_ref[0:DN_HALO, :] = jnp.where(i > 0, halo_ref[0].astype(F32), 0.0)
    xs_ref[DN_HALO:DN_HALO + tc, :] = x_ref[0].astype(F32)
    for cb in range(3 * nh):
        cols = slice(hd * cb, hd * (cb + 1))
        y = None
        for j in range(DN_CONV):
            tap = xs_ref[pl.ds(DN_HALO - (DN_CONV - 1) + j, tc), cols] * cw_ref[j:j + 1, cols]
            y = tap if y is None else y + tap
        y = y * jax.nn.sigmoid(y)
        if cb < 2 * nh:
            y = y * lax.rsqrt(jnp.sum(y * y, axis=-1, keepdims=True) + 1e-6)
        if cb < nh:
            q_s[:, cols] = y * (hd ** -0.5)
        elif cb < 2 * nh:
            k_s[:, hd * (cb - nh):hd * (cb - nh + 1)] = y
        else:
            v_s[:, hd * (cb - 2 * nh):hd * (cb - 2 * nh + 1)] = y

    ab = ab_ref[0]
    lane = lax.broadcasted_iota(jnp.int32, ab.shape, 1)
    xg = ab + prm_ref[1:2, :]
    softplus = jnp.maximum(xg, 0.0) + jnp.log1p(jnp.exp(-jnp.abs(xg)))
    g_s[...] = jnp.where(lane < nh, -jnp.exp(prm_ref[0:1, :]) * softplus, 0.0)
    b_s[...] = jax.nn.sigmoid(ab)

    cat = nh * ck
    ri = lax.broadcasted_iota(jnp.int32, (ck, ck), 0)
    ci = lax.broadcasted_iota(jnp.int32, (ck, ck), 1)
    tri = jnp.where(ri >= ci, 1.0, 0.0).astype(BF16)
    ones = jnp.ones((ck, ck), BF16)
    ii = lax.broadcasted_iota(jnp.int32, (ck, cat), 0)
    lane_c = lax.broadcasted_iota(jnp.int32, (ck, cat), 1)
    jj = lane_c % ck
    blk_c = lane_c // ck
    blk_k = lax.broadcasted_iota(jnp.int32, (ck, nh * hd), 1) // hd
    og = og_ref[...]

    def chunk(c, carry):
        r0 = pl.multiple_of(c * ck, ck)
        rows = pl.ds(r0, ck)
        qc, kc, vc = q_s[rows, :], k_s[rows, :], v_s[rows, :]
        gc = _dot_split3(tri, g_s[rows, :])
        bch = b_s[rows, :]
        glast = gc[ck - 1:ck, :]

        def cols_to_heads(a, lane0, width):
            return jnp.concatenate(
                [jnp.broadcast_to(a[:, lane0 + h:lane0 + h + 1], (a.shape[0], width)) for h in range(nh)], axis=1)

        gc_w = cols_to_heads(gc, 0, hd)
        beta_w = cols_to_heads(bch, nh, hd)
        glast_w = cols_to_heads(glast, 0, hd)
        eg = jnp.exp(gc_w)
        kb = kc * beta_w
        vb = vc * beta_w
        qg = qc * eg
        kbg = kb * eg
        kd = kc * jnp.exp(glast_w - gc_w)

        colcat = jnp.zeros((ck, cat), F32)
        for h in range(nh):
            colcat = jnp.where(blk_c == h, jnp.broadcast_to(gc[:, h:h + 1], (ck, cat)), colcat)
        rowcat = _dot_split3(ones, jnp.where(ii == jj, colcat, 0.0))
        diff = colcat - rowcat
        d_inc = jnp.exp(jnp.where(ii >= jj, diff, -jnp.inf))
        d_str = jnp.where(ii > jj, d_inc, 0.0)

        lhs = jnp.concatenate([kb, qc], axis=0).astype(BF16)
        bdk = jnp.concatenate([jnp.where(blk_k == h, kc, 0.0) for h in range(nh)], axis=0).astype(BF16)
        aq = _nt(lhs, bdk)
        low = aq[:ck] * d_str
        qk = aq[ck:] * d_inc

        m = -low
        p = low
        bdp = _block_diag(p, nh, ck).astype(BF16)
        n_sq = ck.bit_length() - 2
        for _ in range(n_sq):
            p = _dot(p.astype(BF16), bdp)
            bdp = _block_diag(p, nh, ck).astype(BF16)
            m = m + p + _dot(m.astype(BF16), bdp)

        rhs = jnp.concatenate(
            [jnp.concatenate([vb[:, hd * h:hd * (h + 1)], kbg[:, hd * h:hd * (h + 1)]], axis=1) for h in range(nh)],
            axis=0)
        sol = rhs + _dot(_block_diag(m, nh, ck).astype(BF16), rhs.astype(BF16))

        v_new, q_state = [], []
        for h in range(nh):
            u = sol[ck * h:ck * (h + 1), :hd]
            w = sol[ck * h:ck * (h + 1), hd:]
            wq = jnp.concatenate([w, qg[:, hd * h:hd * (h + 1)]], axis=0).astype(BF16)
            r1 = _dot(wq, s_ref[h].astype(BF16))
            v_new.append(u - r1[:ck])
            q_state.append(r1[ck:])
        o_intra = _dot(_block_diag(qk, nh, ck).astype(BF16), jnp.concatenate(v_new, axis=0).astype(BF16))
        for h in range(nh):
            vn = v_new[h].astype(BF16)
            ds = lax.dot_general(kd[:, hd * h:hd * (h + 1)].astype(BF16), vn, (((0,), (0,)), ((), ())),
                                 preferred_element_type=F32)
            s_ref[h] = s_ref[h] * jnp.exp(glast[:, h:h + 1]) + ds
            o = q_state[h] + o_intra[ck * h:ck * (h + 1)]
            o = o * lax.rsqrt(jnp.mean(o * o, axis=-1, keepdims=True) + RMS_EPS) * og
            y_ref[0, rows, hd * h:hd * (h + 1)] = o.astype(y_ref.dtype)
        return carry

    lax.fori_loop(0, tc // ck, chunk, 0)


def _deltanet(dnqkv, dnab, conv_w, a_log, dt_bias, out_norm_g):
    b, s, _ = dnqkv.shape
    tc = min(s, 512)
    hb = tc // DN_HALO
    prm = jnp.zeros((8, LANE), F32).at[0, :DN_HEADS].set(a_log).at[1, :DN_HEADS].set(dt_bias)
    return pl.pallas_call(
        functools.partial(_dn_kernel, tc=tc),
        grid=(b, s // tc),
        in_specs=[
            pl.BlockSpec((1, tc, W_DNQKV), lambda bi, i: (bi, i, 0)),
            pl.BlockSpec((1, DN_HALO, W_DNQKV), lambda bi, i: (bi, jnp.maximum(i * hb - 1, 0), 0)),
            pl.BlockSpec((1, tc, LANE), lambda bi, i: (bi, i, 0)),
            pl.BlockSpec((DN_CONV, W_DNQKV), lambda bi, i: (0, 0)),
            pl.BlockSpec((8, LANE), lambda bi, i: (0, 0)),
            pl.BlockSpec((1, DN_HD), lambda bi, i: (0, 0)),
        ],
        out_specs=pl.BlockSpec((1, tc, DN_WIDTH), lambda bi, i: (bi, i, 0)),
        out_shape=jax.ShapeDtypeStruct((b, s, DN_WIDTH), BF16),
        scratch_shapes=[
            pltpu.VMEM((DN_HEADS, DN_HD, DN_HD), F32),
            pltpu.VMEM((DN_HALO + tc, W_DNQKV), F32),
            pltpu.VMEM((tc, DN_WIDTH), F32), pltpu.VMEM((tc, DN_WIDTH), F32), pltpu.VMEM((tc, DN_WIDTH), F32),
            pltpu.VMEM((tc, LANE), F32), pltpu.VMEM((tc, LANE), F32),
        ],
        compiler_params=_cparams("parallel", "arbitrary"),
        name="gated_deltanet",
    )(dnqkv, dnqkv, dnab, conv_w, prm, out_norm_g.reshape(1, DN_HD))


_DIL_PREP_ROWS = 512


def _dil_kernel(q_ref, k_ref, v_ref, cos_ref, sin_ref, gq_ref, gk_ref, o_ref,
                qf, kf, vf, m_run, l_run, acc, *, seq):
    g = pl.program_id(2)
    blk = DIL_BLOCK
    rt = min(seq, _DIL_PREP_ROWS)

    def prep(t, carry):
        rows = pl.ds(pl.multiple_of(t * rt, rt), rt)
        cs, sn = cos_ref[0, rows, :], sin_ref[0, rows, :]

        def norm_rope(x, gain):
            x = x * lax.rsqrt(jnp.mean(x * x, axis=-1, keepdims=True) + RMS_EPS) * gain
            return x * cs + pltpu.roll(x, DIL_HD // 2, 1) * sn

        qf[rows, :] = norm_rope(q_ref[0, rows, :].astype(F32), gq_ref[...]) * (DIL_HD ** -0.5)
        kf[rows, :] = norm_rope(k_ref[0, rows, :].astype(F32), gk_ref[...])
        vf[rows, :] = v_ref[0, rows, :].astype(F32)
        return carry

    lax.fori_loop(0, seq // rt, prep, 0)

    qi = lax.broadcasted_iota(jnp.int32, (blk, blk), 0)
    kj = lax.broadcasted_iota(jnp.int32, (blk, blk), 1)

    def group(dil, first):
        nb = seq // (dil * blk)

        def rows_at(start):
            return pl.ds(start, blk) if dil == 1 else pl.ds(start, blk, stride=dil)

        def block(idx, carry):
            r = idx // nb
            n = idx % nb
            start = n * (blk * dil) + r
            rows = rows_at(start)
            prow = rows_at(jnp.maximum(start - blk * dil, r))
            qb = qf[rows, :].astype(BF16)
            s_c = _nt(qb, kf[rows, :].astype(BF16))
            s_p = _nt(qb, kf[prow, :].astype(BF16))
            s_c = jnp.where(kj <= qi, s_c, -jnp.inf)
            s_p = jnp.where(jnp.logical_and(kj >= qi, n > 0), s_p, -jnp.inf)
            m_b = jnp.maximum(jnp.max(s_c, axis=-1, keepdims=True), jnp.max(s_p, axis=-1, keepdims=True))
            p_c = jnp.exp(s_c - m_b)
            p_p = jnp.exp(s_p - m_b)
            l_b = jnp.sum(p_c, axis=-1, keepdims=True) + jnp.sum(p_p, axis=-1, keepdims=True)
            pv = _dot(p_c.astype(BF16), vf[rows, :].astype(BF16)) + _dot(p_p.astype(BF16), vf[prow, :].astype(BF16))
            if first:
                m_run[rows, :] = jnp.broadcast_to(m_b, (blk, DIL_HD))
                l_run[rows, :] = jnp.broadcast_to(l_b, (blk, DIL_HD))
                acc[rows, :] = pv
            else:
                m_old = m_run[rows, :]
                m_new = jnp.maximum(m_old, m_b)
                a_old = jnp.exp(m_old - m_new)
                a_b = jnp.exp(m_b - m_new)
                m_run[rows, :] = m_new
                l_run[rows, :] = l_run[rows, :] * a_old + l_b * a_b
                acc[rows, :] = acc[rows, :] * a_old + pv * a_b
            return carry

        lax.fori_loop(0, seq // blk, block, 0)

    for gi, dil in enumerate(DIL_DILATIONS):
        pl.when(g == gi)(functools.partial(group, dil, gi == 0))

    @pl.when(g == DIL_GROUPS - 1)
    def _():
        def fin(t, carry):
            rows = pl.ds(pl.multiple_of(t * rt, rt), rt)
            o_ref[0, rows, :] = (acc[rows, :] / l_run[rows, :]).astype(o_ref.dtype)
            return carry

        lax.fori_loop(0, seq // rt, fin, 0)


def _dilated_attention(dil, cos_h, sin_h, gq, gk):
    b, s, _ = dil.shape
    nheads = DIL_GROUPS * DIL_HPG
    assert all(w // d == DIL_BLOCK for w, d in zip(DIL_WINDOWS, DIL_DILATIONS))
    assert s % (max(DIL_DILATIONS) * DIL_BLOCK) == 0
    part = lambda p: pl.BlockSpec((1, s, DIL_HD), lambda bi, h, g: (bi, 0, p * nheads + g * DIL_HPG + h))
    tab = pl.BlockSpec((1, s, DIL_HD), lambda bi, h, g: (bi, 0, 0))
    gain = pl.BlockSpec((1, DIL_HD), lambda bi, h, g: (0, 0))
    scr = pltpu.VMEM((s, DIL_HD), F32)
    return pl.pallas_call(
        functools.partial(_dil_kernel, seq=s),
        grid=(b, DIL_HPG, DIL_GROUPS),
        in_specs=[part(0), part(1), part(2), tab, tab, gain, gain],
        out_specs=pl.BlockSpec((1, s, DIL_HD), lambda bi, h, g: (bi, 0, h)),
        out_shape=jax.ShapeDtypeStruct((b, s, DIL_HPG * DIL_HD), BF16),
        scratch_shapes=[scr] * 6,
        compiler_params=_cparams("parallel", "parallel", "arbitrary"),
        name="dilated_attention",
    )(dil, dil, dil, cos_h, sin_h, gq.reshape(1, DIL_HD), gk.reshape(1, DIL_HD))


def _merge_kernel(x_ref, ya_ref, yb_ref, yc_ref, z_ref, gate_ref, wb_ref, wo_ref, o_ref):
    d = x_ref.shape[-1]
    mixed = None
    for n, y_ref in enumerate((ya_ref, yb_ref, yc_ref)):
        z = z_ref[:, BRANCH_WIDTH * n:BRANCH_WIDTH * (n + 1)].astype(F32)
        ys = (y_ref[...].astype(F32) * (z * jax.nn.sigmoid(z))).astype(BF16)
        branch = _dot(ys, wb_ref[n])
        gate = jax.nn.sigmoid(gate_ref[:, d * n:d * (n + 1)].astype(F32))
        mixed = gate * branch if mixed is None else mixed + gate * branch
    o_ref[...] = x_ref[...] + _dot(mixed.astype(BF16), wo_ref[...])


def _merge(x2, ya, yb, yc, z, gates, w_branch, w_out):
    t, d = x2.shape
    tm = min(t, 512)
    row = lambda w: pl.BlockSpec((tm, w), lambda i: (i, 0))
    return pl.pallas_call(
        _merge_kernel,
        grid=(t // tm,),
        in_specs=[row(d), row(BRANCH_WIDTH), row(BRANCH_WIDTH), row(BRANCH_WIDTH), row(W_Z), row(N_BRANCHES * d),
                  pl.BlockSpec(w_branch.shape, lambda i: (0, 0, 0)), pl.BlockSpec(w_out.shape, lambda i: (0, 0))],
        out_specs=row(d),
        out_shape=jax.ShapeDtypeStruct((t, d), F32),
        compiler_params=_cparams("parallel"),
        name="merge_out",
    )(x2, ya, yb, yc, z, gates, w_branch, w_out)


def _prep_w_in(w_in):
    d = w_in.shape[-2]
    lead = w_in.shape[:-1]
    o_q = 0
    o_kv = o_q + MLA_Q_RANK
    o_za = o_kv + MLA_KV_RANK + MLA_ROPE
    o_dn = o_za + BRANCH_WIDTH
    o_a = o_dn + 3 * DN_WIDTH
    o_zb = o_a + 2 * DN_HEADS
    o_dil = o_zb + BRANCH_WIDTH
    o_zc = o_dil + 3 * DIL_QKV_WIDTH
    o_g = o_zc + BRANCH_WIDTH
    end = o_g + N_BRANCHES * d
    assert end == w_in.shape[-1]
    zeros = lambda n: jnp.zeros(lead + (n,), w_in.dtype)
    segs = [
        w_in[..., o_q:o_kv],
        w_in[..., o_kv:o_za], zeros(W_KVLAT - (o_za - o_kv)),
        w_in[..., o_dn:o_a],
        w_in[..., o_a:o_zb], zeros(W_DNAB - 2 * DN_HEADS),
        w_in[..., o_dil:o_zc],
        w_in[..., o_za:o_dn], w_in[..., o_zb:o_dil], w_in[..., o_zc:o_g],
        w_in[..., o_g:end],
    ]
    return jnp.concatenate(segs, axis=-1).astype(BF16)


def _prep_w_q_b(w):
    lead = w.shape[:-1]
    w = w.reshape(lead + (MLA_HEADS, MLA_QK))
    pad = jnp.zeros(lead + (MLA_HEADS, MLA_HEAD_PAD - MLA_QK), w.dtype)
    return jnp.concatenate([w, pad], axis=-1).reshape(lead + (MLA_HEADS * MLA_HEAD_PAD,)).astype(BF16)


def _prep_w_kv_b(w):
    lead = w.shape[:-1]
    w = w.reshape(lead + (MLA_HEADS, MLA_NOPE + MLA_V))
    k = w[..., :MLA_NOPE].reshape(lead + (MLA_HEADS * MLA_NOPE,))
    v = w[..., MLA_NOPE:].reshape(lead + (MLA_HEADS * MLA_V,))
    return jnp.concatenate([k, v], axis=-1).astype(BF16)


def _prep_qk_gain(g):
    pad = jnp.zeros(g.shape[:-1] + (LANE - MLA_ROPE,), g.dtype)
    return jnp.stack([g[..., :MLA_NOPE], jnp.concatenate([g[..., MLA_NOPE:], pad], axis=-1)], axis=-2)


def kernel(x, positions, norm_g, w_in, mla_q_a_norm_g, mla_w_q_b, mla_kv_a_norm_g, mla_w_kv_b, mla_q_norm_g,
           mla_k_norm_g, dn_conv_w, dn_a_log, dn_dt_bias, dn_out_norm_g, dil_q_norm_g, dil_k_norm_g, w_branch,
           w_out):
    b, s, d = x.shape
    t = b * s
    depth = w_in.shape[0]

    cos_h, sin_h, cr, sa, sb = _rope_tables(positions)
    cos_h3 = cos_h.reshape(b, s, LANE)
    sin_h3 = sin_h.reshape(b, s, LANE)

    w_in_p = _prep_w_in(w_in)
    w_q_p = _prep_w_q_b(mla_w_q_b)
    w_kv_p = _prep_w_kv_b(mla_w_kv_b)
    gq_p = _prep_qk_gain(mla_q_norm_g)
    gk_p = _prep_qk_gain(mla_k_norm_g)
    w_branch_b = w_branch.astype(BF16)
    w_out_b = w_out.astype(BF16)

    x2 = x.reshape(t, d)
    for l in range(depth):
        qlat, kvlat, dnqkv, dnab, dil, z, gates = _in_projection(x2, norm_g[l], w_in_p[l])
        q, k, v = _mla_prep(qlat, kvlat, cr, sa, sb, mla_q_a_norm_g[l].reshape(1, -1), w_q_p[l],
                            mla_kv_a_norm_g[l].reshape(1, -1), w_kv_p[l], gq_p[l], gk_p[l])
        y_a = _mla_attention(q.reshape(b, s, -1), k.reshape(b, s, -1), v.reshape(b, s, -1))
        y_b = _deltanet(dnqkv.reshape(b, s, -1), dnab.reshape(b, s, -1), dn_conv_w[l], dn_a_log[l], dn_dt_bias[l],
                        dn_out_norm_g[l])
        y_c = _dilated_attention(dil.reshape(b, s, -1), cos_h3, sin_h3, dil_q_norm_g[l], dil_k_norm_g[l])
        x2 = _merge(x2, y_a.reshape(t, -1), y_b.reshape(t, -1), y_c.reshape(t, -1), z, gates, w_branch_b[l],
                    w_out_b[l])
    return x2.reshape(b, s, d)
```

```python
import functools

import jax
import jax.numpy as jnp
from jax import lax
from jax.experimental import pallas as pl
from jax.experimental.pallas import tpu as pltpu

F32 = jnp.float32
BF16 = jnp.bfloat16

RMS_EPS = 1e-6
ROPE_THETA = 10000.0
LANE = 128

MLA_HEADS = 4
MLA_NOPE = 128
MLA_ROPE = 64
MLA_V = 128
MLA_QK = MLA_NOPE + MLA_ROPE
MLA_Q_RANK = 384
MLA_KV_RANK = 256
MLA_KV_PAD = 384
MLA_HEAD_PAD = 256
MLA_VT_ROWS = MLA_V + 16
LOG2_E = 1.4426950408889634

DN_HEADS = 4
DN_HD = 128
DN_WIDTH = DN_HEADS * DN_HD
DN_CONV = 4
DN_CHUNK = 64
DN_HALO = 16
DN_PREP_UNROLL = 4

DIL_WINDOWS = (128, 512, 2048)
DIL_DILATIONS = (1, 4, 16)
DIL_GROUPS = 3
DIL_HPG = 4
DIL_HD = 128
DIL_BLOCK = 128
DIL_QKV_WIDTH = DIL_GROUPS * DIL_HPG * DIL_HD
DIL_UNROLL = 4

N_BRANCHES = 3
BRANCH_WIDTH = 512

W_QLAT = MLA_Q_RANK
W_KVLAT = MLA_KV_PAD
W_DNQKV = 3 * DN_WIDTH
W_DNAB = LANE
W_DIL = 3 * DIL_QKV_WIDTH
W_Z = N_BRANCHES * BRANCH_WIDTH

VMEM_LIMIT = 56 * 1024 * 1024


def _cparams(*sem):
    return pltpu.CompilerParams(dimension_semantics=sem, vmem_limit_bytes=VMEM_LIMIT)


def _nt(a, b):
    return lax.dot_general(a, b, (((1,), (1,)), ((), ())), preferred_element_type=F32)


def _dot(a, b):
    return jnp.dot(a, b, preferred_element_type=F32)


def _run_staged(gens, skew):
    results = [None] * len(gens)
    done = [False] * len(gens)
    t = 0
    while not all(done):
        for n, gen in enumerate(gens):
            if n * skew <= t and not done[n]:
                try:
                    next(gen)
                except StopIteration as stop:
                    results[n] = stop.value
                    done[n] = True
        t += 1
    return results


def _dot_split3(a, x):
    hi = x.astype(BF16)
    r1 = x - hi.astype(F32)
    mid = r1.astype(BF16)
    lo = (r1 - mid.astype(F32)).astype(BF16)
    return _dot(a, hi) + _dot(a, mid) + _dot(a, lo)


def _rope_kernel(pos_ref, c_ref, cosh_ref, sinh_ref, cr_ref, sa_ref, sb_ref):
    pos = pos_ref[...].astype(F32)
    c = c_ref[...]
    ang_h = pos * c[0:1, :]
    cosh_ref[...] = jnp.cos(ang_h)
    sinh_ref[...] = jnp.sin(ang_h) * c[1:2, :]
    ang_r = pos * c[2:3, :]
    sr = jnp.sin(ang_r)
    cr_ref[...] = jnp.cos(ang_r) * c[3:4, :]
    sa_ref[...] = sr * c[4:5, :]
    sb_ref[...] = sr * c[5:6, :]


def _rope_tables(positions):
    t = positions.size
    ts = min(t, 1024)
    half_h = DIL_HD // 2
    half_r = MLA_ROPE // 2
    inv_h = 1.0 / (ROPE_THETA ** (jnp.arange(0, DIL_HD, 2, dtype=F32) / DIL_HD))
    inv_r = 1.0 / (ROPE_THETA ** (jnp.arange(0, MLA_ROPE, 2, dtype=F32) / MLA_ROPE))
    zeros = jnp.zeros
    ones = jnp.ones
    rows = [
        jnp.concatenate([inv_h, inv_h]),
        jnp.concatenate([-ones(half_h, F32), ones(half_h, F32)]),
        jnp.concatenate([inv_r, inv_r, zeros(LANE - MLA_ROPE, F32)]),
        jnp.concatenate([ones(MLA_ROPE, F32), zeros(LANE - MLA_ROPE, F32)]),
        jnp.concatenate([-ones(half_r, F32), zeros(LANE - half_r, F32)]),
        jnp.concatenate([zeros(half_r, F32), ones(half_r, F32), zeros(LANE - MLA_ROPE, F32)]),
        zeros(LANE, F32),
        zeros(LANE, F32),
    ]
    consts = jnp.stack(rows)
    tab = jax.ShapeDtypeStruct((t, LANE), F32)
    spec = pl.BlockSpec((ts, LANE), lambda i: (i, 0))
    return pl.pallas_call(
        _rope_kernel,
        grid=(t // ts,),
        in_specs=[pl.BlockSpec((ts, 1), lambda i: (i, 0)), pl.BlockSpec((8, LANE), lambda i: (0, 0))],
        out_specs=[spec] * 5,
        out_shape=[tab] * 5,
        compiler_params=_cparams("parallel"),
        name="rope_tables",
    )(positions.reshape(t, 1), consts)


_INPROJ_COLS = 512


def _inproj_kernel(x_ref, g_ref, w_ref, *out_refs):
    x = x_ref[...]
    h = (x * lax.rsqrt(jnp.mean(x * x, axis=-1, keepdims=True) + RMS_EPS) * g_ref[...]).astype(BF16)
    off = 0
    for o_ref in out_refs:
        width = o_ref.shape[-1]
        for c0 in range(0, width, _INPROJ_COLS):
            cw = min(_INPROJ_COLS, width - c0)
            acc = _dot(h, w_ref[:, off + c0:off + c0 + cw])
            o_ref[:, c0:c0 + cw] = acc.astype(o_ref.dtype)
        off += width


def _in_projection(x2, norm_g, w_in_p):
    t, d = x2.shape
    tm = min(t, 256)
    widths = (W_QLAT, W_KVLAT, W_DNQKV, W_DNAB, W_DIL, W_Z, N_BRANCHES * d)
    dtypes = (BF16, BF16, BF16, F32, BF16, BF16, BF16)
    assert sum(widths) == w_in_p.shape[1]
    return pl.pallas_call(
        _inproj_kernel,
        grid=(t // tm,),
        in_specs=[
            pl.BlockSpec((tm, d), lambda i: (i, 0)),
            pl.BlockSpec((1, d), lambda i: (0, 0)),
            pl.BlockSpec(memory_space=pltpu.VMEM),
        ],
        out_specs=[pl.BlockSpec((tm, w), lambda i: (i, 0)) for w in widths],
        out_shape=[jax.ShapeDtypeStruct((t, w), dt) for w, dt in zip(widths, dtypes)],
        compiler_params=_cparams("parallel"),
        name="in_projection",
    )(x2, norm_g.reshape(1, d), w_in_p)


def _mla_prep_kernel(qlat_ref, kvlat_ref, cr_ref, sa_ref, sb_ref, gqa_ref, wq_ref, gkva_ref, wk_ref, wvt_ref,
                     gq_ref, gk_ref, q_ref, k_ref, vt_ref):
    def rms(x, gain, n):
        return x * lax.rsqrt(jnp.sum(x * x, axis=-1, keepdims=True) * (1.0 / n) + RMS_EPS) * gain

    cr, sa, sb = cr_ref[...], sa_ref[...], sb_ref[...]

    def rope(x):
        return x * cr + pltpu.roll(x, LANE - MLA_ROPE // 2, 1) * sa + pltpu.roll(x, MLA_ROPE // 2, 1) * sb

    scale = MLA_QK ** -0.5 * LOG2_E
    qn = rms(qlat_ref[...].astype(F32), gqa_ref[...], MLA_Q_RANK).astype(BF16)
    q = _dot(qn, wq_ref[...])
    kvl = kvlat_ref[...].astype(F32)
    cn = rms(kvl[:, :MLA_KV_RANK], gkva_ref[...], MLA_KV_RANK).astype(BF16)
    kn = _dot(cn, wk_ref[...])
    vt = _nt(wvt_ref[...], cn).astype(BF16)
    for h in range(MLA_HEADS):
        r0 = MLA_VT_ROWS * h
        vt_ref[0, r0:r0 + MLA_V, :] = vt[MLA_V * h:MLA_V * (h + 1)]
        vt_ref[0, r0 + MLA_V:r0 + MLA_VT_ROWS, :] = jnp.ones((MLA_VT_ROWS - MLA_V, vt.shape[1]), BF16)
    k_pe = rope(rms(kvl[:, MLA_KV_RANK:], gk_ref[1:2, :], MLA_ROPE)).astype(BF16)
    for h in range(MLA_HEADS):
        c0 = MLA_HEAD_PAD * h
        q_nope = rms(q[:, c0:c0 + MLA_NOPE], gq_ref[0:1, :], MLA_NOPE)
        q_pe = rope(rms(q[:, c0 + MLA_NOPE:c0 + MLA_HEAD_PAD], gq_ref[1:2, :], MLA_ROPE))
        q_ref[:, c0:c0 + MLA_NOPE] = (q_nope * scale).astype(BF16)
        q_ref[:, c0 + MLA_NOPE:c0 + MLA_HEAD_PAD] = (q_pe * scale).astype(BF16)
        k_nope = rms(kn[:, MLA_NOPE * h:MLA_NOPE * (h + 1)], gk_ref[0:1, :], MLA_NOPE)
        k_ref[:, c0:c0 + MLA_NOPE] = k_nope.astype(BF16)
        k_ref[:, c0 + MLA_NOPE:c0 + MLA_HEAD_PAD] = k_pe


def _mla_prep(qlat, kvlat, cr, sa, sb, gqa, wq, gkva, wk, wvt, gq, gk, tm):
    t = qlat.shape[0]
    row = lambda w: pl.BlockSpec((tm, w), lambda i: (i, 0))
    full = lambda a: pl.BlockSpec(a.shape, lambda i: (0,) * a.ndim)
    qk_w = MLA_HEADS * MLA_HEAD_PAD
    v_w = MLA_HEADS * MLA_VT_ROWS
    return pl.pallas_call(
        _mla_prep_kernel,
        grid=(t // tm,),
        in_specs=[row(W_QLAT), row(W_KVLAT), row(LANE), row(LANE), row(LANE),
                  full(gqa), full(wq), full(gkva), full(wk), full(wvt), full(gq), full(gk)],
        out_specs=[row(qk_w), row(qk_w), pl.BlockSpec((1, v_w, tm), lambda i: (i, 0, 0))],
        out_shape=[jax.ShapeDtypeStruct((t, qk_w), BF16), jax.ShapeDtypeStruct((t, qk_w), BF16),
                   jax.ShapeDtypeStruct((t // tm, v_w, tm), BF16)],
        compiler_params=_cparams("parallel"),
        name="mla_prep",
    )(qlat, kvlat, cr, sa, sb, gqa, wq, gkva, wk, wvt, gq, gk)


def _mla_attn_kernel(q_ref, k_ref, vt_ref, o_ref, m_ref, acc_ref, *, tq):
    qi = pl.program_id(1)
    m_ref[...] = jnp.full(m_ref.shape, -jnp.inf, F32)
    acc_ref[...] = jnp.zeros(acc_ref.shape, F32)

    def head_step(j, start, h, diagonal):
        cols = slice(MLA_HEAD_PAD * h, MLA_HEAD_PAD * (h + 1))
        st = _nt(k_ref[0, pl.ds(start, tq), cols], q_ref[0, :, cols])
        yield
        if diagonal:
            key = lax.broadcasted_iota(jnp.int32, st.shape, 0)
            qry = lax.broadcasted_iota(jnp.int32, st.shape, 1)
            st = jnp.where(key <= qry, st, -jnp.inf)
        m_old = m_ref[h]
        m_new = jnp.maximum(m_old, jnp.max(st, axis=0, keepdims=True))
        alpha = jnp.exp2(m_old - m_new)
        p = jnp.exp2(st - m_new).astype(BF16)
        yield
        vt = vt_ref[0, j, MLA_VT_ROWS * h:MLA_VT_ROWS * (h + 1), :]
        acc_ref[h] = alpha * acc_ref[h] + _dot(vt, p)
        m_ref[h] = m_new

    def step(j, diagonal):
        start = pl.multiple_of(j * tq, tq)
        _run_staged([head_step(j, start, h, diagonal) for h in range(MLA_HEADS)], skew=1)

    def body(j, carry):
        step(j, False)
        return carry

    lax.fori_loop(0, qi, body, 0)
    step(qi, True)
    for h in range(MLA_HEADS):
        o = acc_ref[h, :MLA_V, :] / acc_ref[h, MLA_V:MLA_V + 1, :]
        o_ref[0, :, MLA_V * h:MLA_V * (h + 1)] = o.T.astype(o_ref.dtype)


def _mla_attention(q, k, vt, tq):
    b, s, qk_w = q.shape
    v_w = MLA_HEADS * MLA_V
    vt_w = MLA_HEADS * MLA_VT_ROWS
    return pl.pallas_call(
        functools.partial(_mla_attn_kernel, tq=tq),
        grid=(b, s // tq),
        in_specs=[
            pl.BlockSpec((1, tq, qk_w), lambda bi, i: (bi, i, 0)),
            pl.BlockSpec((1, s, qk_w), lambda bi, i: (bi, 0, 0)),
            pl.BlockSpec((1, s // tq, vt_w, tq), lambda bi, i: (bi, 0, 0, 0)),
        ],
        out_specs=pl.BlockSpec((1, tq, v_w), lambda bi, i: (bi, i, 0)),
        out_shape=jax.ShapeDtypeStruct((b, s, v_w), BF16),
        scratch_shapes=[pltpu.VMEM((MLA_HEADS, 1, tq), F32), pltpu.VMEM((MLA_HEADS, MLA_VT_ROWS, tq), F32)],
        compiler_params=_cparams("parallel", "arbitrary"),
        name="mla_attention",
    )(q, k, vt)


def _block_diag(x, nblk, bw):
    blk = lax.broadcasted_iota(jnp.int32, x.shape, 1) // bw
    return jnp.concatenate([jnp.where(blk == h, x, 0.0) for h in range(nblk)], axis=0)


def _dn_kernel(x_ref, halo_ref, ab_ref, cw_ref, prm_ref, og_ref, y_ref,
               s_ref, xs_ref, q_s, k_s, v_s, g_s, b_s, u_s, wq_s, bdqk_s, kd_s, egl_s, *, tc):
    i = pl.program_id(1)
    nh, hd, ck = DN_HEADS, DN_HD, DN_CHUNK

    @pl.when(i == 0)
    def _():
        s_ref[...] = jnp.zeros(s_ref.shape, F32)

    xs_ref[0:DN_HALO, :] = jnp.where(i > 0, halo_ref[0].astype(F32), 0.0)
    xs_ref[DN_HALO:DN_HALO + tc, :] = x_ref[0].astype(F32)
    for cb in range(3 * nh):
        cols = slice(hd * cb, hd * (cb + 1))
        y = None
        for j in range(DN_CONV):
            tap = xs_ref[pl.ds(DN_HALO - (DN_CONV - 1) + j, tc), cols] * cw_ref[j:j + 1, cols]
            y = tap if y is None else y + tap
        y = y * jax.nn.sigmoid(y)
        if cb < 2 * nh:
            y = y * lax.rsqrt(jnp.sum(y * y, axis=-1, keepdims=True) + 1e-6)
        if cb < nh:
            q_s[:, cols] = y * (hd ** -0.5)
        elif cb < 2 * nh:
            k_s[:, hd * (cb - nh):hd * (cb - nh + 1)] = y
        else:
            v_s[:, hd * (cb - 2 * nh):hd * (cb - 2 * nh + 1)] = y

    ab = ab_ref[0]
    lane = lax.broadcasted_iota(jnp.int32, ab.shape, 1)
    xg = ab + prm_ref[1:2, :]
    softplus = jnp.maximum(xg, 0.0) + jnp.log1p(jnp.exp(-jnp.abs(xg)))
    g_s[...] = jnp.where(lane < nh, -jnp.exp(prm_ref[0:1, :]) * softplus, 0.0)
    b_s[...] = jax.nn.sigmoid(ab)

    cat = nh * ck
    ri = lax.broadcasted_iota(jnp.int32, (ck, ck), 0)
    ci = lax.broadcasted_iota(jnp.int32, (ck, ck), 1)
    tri = jnp.where(ri >= ci, 1.0, 0.0).astype(BF16)
    ones = jnp.ones((ck, ck), BF16)
    ii = lax.broadcasted_iota(jnp.int32, (ck, cat), 0)
    lane_c = lax.broadcasted_iota(jnp.int32, (ck, cat), 1)
    jj = lane_c % ck
    blk_c = lane_c // ck
    blk_k = lax.broadcasted_iota(jnp.int32, (ck, nh * hd), 1) // hd
    og = og_ref[...]

    def prep_chunk(c):
        r0 = pl.multiple_of(c * ck, ck)
        rows = pl.ds(r0, ck)
        qc, kc, vc = q_s[rows, :], k_s[rows, :], v_s[rows, :]
        gc = _dot_split3(tri, g_s[rows, :])
        yield
        bch = b_s[rows, :]
        glast = gc[ck - 1:ck, :]

        def cols_to_heads(a, lane0, width):
            return jnp.concatenate(
                [jnp.broadcast_to(a[:, lane0 + h:lane0 + h + 1], (a.shape[0], width)) for h in range(nh)], axis=1)

        gc_w = cols_to_heads(gc, 0, hd)
        beta_w = cols_to_heads(bch, nh, hd)
        glast_w = cols_to_heads(glast, 0, hd)
        eg = jnp.exp(gc_w)
        kb = kc * beta_w
        vb = vc * beta_w
        qg = qc * eg
        kbg = kb * eg
        kd = kc * jnp.exp(glast_w - gc_w)

        colcat = jnp.zeros((ck, cat), F32)
        for h in range(nh):
            colcat = jnp.where(blk_c == h, jnp.broadcast_to(gc[:, h:h + 1], (ck, cat)), colcat)
        rowcat = _dot_split3(ones, jnp.where(ii == jj, colcat, 0.0))
        yield
        diff = colcat - rowcat
        d_inc = jnp.exp(jnp.where(ii >= jj, diff, -jnp.inf))
        d_str = jnp.where(ii > jj, d_inc, 0.0)

        lhs = jnp.concatenate([kb, qc], axis=0).astype(BF16)
        bdk = jnp.concatenate([jnp.where(blk_k == h, kc, 0.0) for h in range(nh)], axis=0).astype(BF16)
        aq = _nt(lhs, bdk)
        yield
        low = aq[:ck] * d_str
        qk = aq[ck:] * d_inc

        m = -low
        p = low
        bdp = _block_diag(p, nh, ck).astype(BF16)
        n_sq = ck.bit_length() - 2
        for _ in range(n_sq):
            p = _dot(p.astype(BF16), bdp)
            yield
            bdp = _block_diag(p, nh, ck).astype(BF16)
            m = m + p + _dot(m.astype(BF16), bdp)
        yield

        rhs = jnp.concatenate(
            [jnp.concatenate([vb[:, hd * h:hd * (h + 1)], kbg[:, hd * h:hd * (h + 1)]], axis=1) for h in range(nh)],
            axis=0)
        sol = rhs + _dot(_block_diag(m, nh, ck).astype(BF16), rhs.astype(BF16))
        wq = [jnp.concatenate([sol[ck * h:ck * (h + 1), hd:], qg[:, hd * h:hd * (h + 1)]], axis=0).astype(BF16)
              for h in range(nh)]
        return (sol[:, :hd], wq, _block_diag(qk, nh, ck).astype(BF16), kd.astype(BF16),
                jnp.broadcast_to(jnp.exp(glast), (8, LANE)))

    def prep_group(cg, carry):
        chunks = [cg * DN_PREP_UNROLL + un for un in range(DN_PREP_UNROLL)]
        results = _run_staged([prep_chunk(c) for c in chunks], skew=0)
        for c, (u, wq, bdqk, kd, egl) in zip(chunks, results):
            u_s[c] = u
            for h in range(nh):
                wq_s[c, h] = wq[h]
            bdqk_s[c] = bdqk
            kd_s[c] = kd
            egl_s[c] = egl
        return carry

    lax.fori_loop(0, tc // (ck * DN_PREP_UNROLL), prep_group, 0)

    def recur_chunk(c, carry):
        rows = pl.ds(pl.multiple_of(c * ck, ck), ck)
        v_new, q_state = [], []
        for h in range(nh):
            r1 = _dot(wq_s[c, h], s_ref[h].astype(BF16))
            v_new.append(u_s[c, ck * h:ck * (h + 1), :] - r1[:ck])
            q_state.append(r1[ck:])
        o_intra = _dot(bdqk_s[c], jnp.concatenate(v_new, axis=0).astype(BF16))
        egl = egl_s[c]
        for h in range(nh):
            ds = lax.dot_general(kd_s[c, :, hd * h:hd * (h + 1)], v_new[h].astype(BF16), (((0,), (0,)), ((), ())),
                                 preferred_element_type=F32)
            s_ref[h] = s_ref[h] * egl[0:1, h:h + 1] + ds
            o = q_state[h] + o_intra[ck * h:ck * (h + 1)]
            o = o * lax.rsqrt(jnp.mean(o * o, axis=-1, keepdims=True) + RMS_EPS) * og
            y_ref[0, rows, hd * h:hd * (h + 1)] = o.astype(y_ref.dtype)
        return carry

    lax.fori_loop(0, tc // ck, recur_chunk, 0)


def _deltanet(dnqkv, dnab, conv_w, a_log, dt_bias, out_norm_g):
    b, s, _ = dnqkv.shape
    tc = min(s, 512)
    hb = tc // DN_HALO
    nck = tc // DN_CHUNK
    assert nck % DN_PREP_UNROLL == 0
    prm = jnp.zeros((8, LANE), F32).at[0, :DN_HEADS].set(a_log).at[1, :DN_HEADS].set(dt_bias)
    return pl.pallas_call(
        functools.partial(_dn_kernel, tc=tc),
        grid=(b, s // tc),
        in_specs=[
            pl.BlockSpec((1, tc, W_DNQKV), lambda bi, i: (bi, i, 0)),
            pl.BlockSpec((1, DN_HALO, W_DNQKV), lambda bi, i: (bi, jnp.maximum(i * hb - 1, 0), 0)),
            pl.BlockSpec((1, tc, LANE), lambda bi, i: (bi, i, 0)),
            pl.BlockSpec((DN_CONV, W_DNQKV), lambda bi, i: (0, 0)),
            pl.BlockSpec((8, LANE), lambda bi, i: (0, 0)),
            pl.BlockSpec((1, DN_HD), lambda bi, i: (0, 0)),
        ],
        out_specs=pl.BlockSpec((1, tc, DN_WIDTH), lambda bi, i: (bi, i, 0)),
        out_shape=jax.ShapeDtypeStruct((b, s, DN_WIDTH), BF16),
        scratch_shapes=[
            pltpu.VMEM((DN_HEADS, DN_HD, DN_HD), F32),
            pltpu.VMEM((DN_HALO + tc, W_DNQKV), F32),
            pltpu.VMEM((tc, DN_WIDTH), F32), pltpu.VMEM((tc, DN_WIDTH), F32), pltpu.VMEM((tc, DN_WIDTH), F32),
            pltpu.VMEM((tc, LANE), F32), pltpu.VMEM((tc, LANE), F32),
            pltpu.VMEM((nck, DN_HEADS * DN_CHUNK, DN_HD), F32),
            pltpu.VMEM((nck, DN_HEADS, 2 * DN_CHUNK, DN_HD), BF16),
            pltpu.VMEM((nck, DN_HEADS * DN_CHUNK, DN_HEADS * DN_CHUNK), BF16),
            pltpu.VMEM((nck, DN_CHUNK, DN_WIDTH), BF16),
            pltpu.VMEM((nck, 8, LANE), F32),
        ],
        compiler_params=_cparams("parallel", "arbitrary"),
        name="gated_deltanet",
    )(dnqkv, dnqkv, dnab, conv_w, prm, out_norm_g.reshape(1, DN_HD))


_DIL_PREP_ROWS = 512


def _dil_kernel(q_ref, k_ref, v_ref, cos_ref, sin_ref, gq_ref, gk_ref, o_ref,
                qf, kf, vf, m_run, l_run, acc, *, seq):
    g = pl.program_id(2)
    blk = DIL_BLOCK
    rt = min(seq, _DIL_PREP_ROWS)

    def prep(t, carry):
        rows = pl.ds(pl.multiple_of(t * rt, rt), rt)
        cs, sn = cos_ref[0, rows, :], sin_ref[0, rows, :]

        def norm_rope(x, gain):
            x = x * lax.rsqrt(jnp.mean(x * x, axis=-1, keepdims=True) + RMS_EPS) * gain
            return x * cs + pltpu.roll(x, DIL_HD // 2, 1) * sn

        qf[rows, :] = norm_rope(q_ref[0, rows, :].astype(F32), gq_ref[...]) * (DIL_HD ** -0.5)
        kf[rows, :] = norm_rope(k_ref[0, rows, :].astype(F32), gk_ref[...])
        vf[rows, :] = v_ref[0, rows, :].astype(F32)
        return carry

    lax.fori_loop(0, seq // rt, prep, 0)

    qi = lax.broadcasted_iota(jnp.int32, (blk, 2 * blk), 0)
    kj = lax.broadcasted_iota(jnp.int32, (blk, 2 * blk), 1)
    band = jnp.logical_and(kj >= qi, kj <= qi + blk)
    prev_half = kj < blk

    def group(dil, first):
        nb = seq // (dil * blk)

        def rows_at(start):
            return pl.ds(start, blk) if dil == 1 else pl.ds(start, blk, stride=dil)

        def block(idx):
            r = idx // nb
            n = idx % nb
            start = n * (blk * dil) + r
            rows = rows_at(start)
            prow = rows_at(jnp.maximum(start - blk * dil, r))
            qb = qf[rows, :].astype(BF16)
            kcat = jnp.concatenate([kf[prow, :], kf[rows, :]], axis=0).astype(BF16)
            vcat = jnp.concatenate([vf[prow, :], vf[rows, :]], axis=0).astype(BF16)
            s = _nt(qb, kcat)
            yield
            no_prev = jnp.where(n > 0, 0.0, -jnp.inf)
            s = jnp.where(band, s + jnp.where(prev_half, no_prev, 0.0), -jnp.inf)
            m_b = jnp.max(s, axis=-1, keepdims=True)
            yield
            p = jnp.exp(s - m_b)
            l_b = jnp.sum(p, axis=-1, keepdims=True)
            pv = _dot(p.astype(BF16), vcat)
            yield
            if first:
                return rows, jnp.broadcast_to(m_b, (blk, DIL_HD)), jnp.broadcast_to(l_b, (blk, DIL_HD)), pv
            m_old = m_run[rows, :]
            m_new = jnp.maximum(m_old, m_b)
            a_old = jnp.exp(m_old - m_new)
            a_b = jnp.exp(m_b - m_new)
            return rows, m_new, l_run[rows, :] * a_old + l_b * a_b, acc[rows, :] * a_old + pv * a_b

        def block_group(ig, carry):
            results = _run_staged([block(ig * DIL_UNROLL + un) for un in range(DIL_UNROLL)], skew=0)
            for rows, m_new, l_new, acc_new in results:
                m_run[rows, :] = m_new
                l_run[rows, :] = l_new
                acc[rows, :] = acc_new
            return carry

        lax.fori_loop(0, seq // (blk * DIL_UNROLL), block_group, 0)

    for gi, dil in enumerate(DIL_DILATIONS):
        pl.when(g == gi)(functools.partial(group, dil, gi == 0))

    @pl.when(g == DIL_GROUPS - 1)
    def _():
        def fin(t, carry):
            rows = pl.ds(pl.multiple_of(t * rt, rt), rt)
            o_ref[0, rows, :] = (acc[rows, :] / l_run[rows, :]).astype(o_ref.dtype)
            return carry

        lax.fori_loop(0, seq // rt, fin, 0)


def _dilated_attention(dil, cos_h, sin_h, gq, gk):
    b, s, _ = dil.shape
    nheads = DIL_GROUPS * DIL_HPG
    assert all(w // d == DIL_BLOCK for w, d in zip(DIL_WINDOWS, DIL_DILATIONS))
    assert s % (max(DIL_DILATIONS) * DIL_BLOCK) == 0
    part = lambda p: pl.BlockSpec((1, s, DIL_HD), lambda bi, h, g: (bi, 0, p * nheads + g * DIL_HPG + h))
    tab = pl.BlockSpec((1, s, DIL_HD), lambda bi, h, g: (bi, 0, 0))
    gain = pl.BlockSpec((1, DIL_HD), lambda bi, h, g: (0, 0))
    scr = pltpu.VMEM((s, DIL_HD), F32)
    return pl.pallas_call(
        functools.partial(_dil_kernel, seq=s),
        grid=(b, DIL_HPG, DIL_GROUPS),
        in_specs=[part(0), part(1), part(2), tab, tab, gain, gain],
        out_specs=pl.BlockSpec((1, s, DIL_HD), lambda bi, h, g: (bi, 0, h)),
        out_shape=jax.ShapeDtypeStruct((b, s, DIL_HPG * DIL_HD), BF16),
        scratch_shapes=[scr] * 6,
        compiler_params=_cparams("parallel", "parallel", "arbitrary"),
        name="dilated_attention",
    )(dil, dil, dil, cos_h, sin_h, gq.reshape(1, DIL_HD), gk.reshape(1, DIL_HD))


def _merge_kernel(x_ref, ya_ref, yb_ref, yc_ref, z_ref, gate_ref, wb_ref, wo_ref, o_ref):
    d = x_ref.shape[-1]
    mixed = None
    for n, y_ref in enumerate((ya_ref, yb_ref, yc_ref)):
        z = z_ref[:, BRANCH_WIDTH * n:BRANCH_WIDTH * (n + 1)].astype(F32)
        ys = (y_ref[...].astype(F32) * (z * jax.nn.sigmoid(z))).astype(BF16)
        branch = _dot(ys, wb_ref[n])
        gate = jax.nn.sigmoid(gate_ref[:, d * n:d * (n + 1)].astype(F32))
        mixed = gate * branch if mixed is None else mixed + gate * branch
    o_ref[...] = x_ref[...] + _dot(mixed.astype(BF16), wo_ref[...])


def _merge(x2, ya, yb, yc, z, gates, w_branch, w_out):
    t, d = x2.shape
    tm = min(t, 512)
    row = lambda w: pl.BlockSpec((tm, w), lambda i: (i, 0))
    return pl.pallas_call(
        _merge_kernel,
        grid=(t // tm,),
        in_specs=[row(d), row(BRANCH_WIDTH), row(BRANCH_WIDTH), row(BRANCH_WIDTH), row(W_Z), row(N_BRANCHES * d),
                  pl.BlockSpec(w_branch.shape, lambda i: (0, 0, 0)), pl.BlockSpec(w_out.shape, lambda i: (0, 0))],
        out_specs=row(d),
        out_shape=jax.ShapeDtypeStruct((t, d), F32),
        compiler_params=_cparams("parallel"),
        name="merge_out",
    )(x2, ya, yb, yc, z, gates, w_branch, w_out)


def _prep_w_in(w_in):
    d = w_in.shape[-2]
    lead = w_in.shape[:-1]
    o_q = 0
    o_kv = o_q + MLA_Q_RANK
    o_za = o_kv + MLA_KV_RANK + MLA_ROPE
    o_dn = o_za + BRANCH_WIDTH
    o_a = o_dn + 3 * DN_WIDTH
    o_zb = o_a + 2 * DN_HEADS
    o_dil = o_zb + BRANCH_WIDTH
    o_zc = o_dil + 3 * DIL_QKV_WIDTH
    o_g = o_zc + BRANCH_WIDTH
    end = o_g + N_BRANCHES * d
    assert end == w_in.shape[-1]
    zeros = lambda n: jnp.zeros(lead + (n,), w_in.dtype)
    segs = [
        w_in[..., o_q:o_kv],
        w_in[..., o_kv:o_za], zeros(W_KVLAT - (o_za - o_kv)),
        w_in[..., o_dn:o_a],
        w_in[..., o_a:o_zb], zeros(W_DNAB - 2 * DN_HEADS),
        w_in[..., o_dil:o_zc],
        w_in[..., o_za:o_dn], w_in[..., o_zb:o_dil], w_in[..., o_zc:o_g],
        w_in[..., o_g:end],
    ]
    return jnp.concatenate(segs, axis=-1).astype(BF16)


def _prep_w_q_b(w):
    lead = w.shape[:-1]
    w = w.reshape(lead + (MLA_HEADS, MLA_QK))
    pad = jnp.zeros(lead + (MLA_HEADS, MLA_HEAD_PAD - MLA_QK), w.dtype)
    return jnp.concatenate([w, pad], axis=-1).reshape(lead + (MLA_HEADS * MLA_HEAD_PAD,)).astype(BF16)


def _prep_w_kv_b(w):
    lead = w.shape[:-1]
    w = w.reshape(lead + (MLA_HEADS, MLA_NOPE + MLA_V))
    k = w[..., :MLA_NOPE].reshape(lead + (MLA_HEADS * MLA_NOPE,))
    v = w[..., MLA_NOPE:].reshape(lead + (MLA_HEADS * MLA_V,))
    return k.astype(BF16), jnp.swapaxes(v, -1, -2).astype(BF16)


def _prep_qk_gain(g):
    pad = jnp.zeros(g.shape[:-1] + (LANE - MLA_ROPE,), g.dtype)
    return jnp.stack([g[..., :MLA_NOPE], jnp.concatenate([g[..., MLA_NOPE:], pad], axis=-1)], axis=-2)


def kernel(x, positions, norm_g, w_in, mla_q_a_norm_g, mla_w_q_b, mla_kv_a_norm_g, mla_w_kv_b, mla_q_norm_g,
           mla_k_norm_g, dn_conv_w, dn_a_log, dn_dt_bias, dn_out_norm_g, dil_q_norm_g, dil_k_norm_g, w_branch,
           w_out):
    b, s, d = x.shape
    t = b * s
    depth = w_in.shape[0]

    cos_h, sin_h, cr, sa, sb = _rope_tables(positions)
    cos_h3 = cos_h.reshape(b, s, LANE)
    sin_h3 = sin_h.reshape(b, s, LANE)

    w_in_p = _prep_w_in(w_in)
    w_q_p = _prep_w_q_b(mla_w_q_b)
    w_k_p, w_vt_p = _prep_w_kv_b(mla_w_kv_b)
    tq = min(s, 512)
    gq_p = _prep_qk_gain(mla_q_norm_g)
    gk_p = _prep_qk_gain(mla_k_norm_g)
    w_branch_b = w_branch.astype(BF16)
    w_out_b = w_out.astype(BF16)

    x2 = x.reshape(t, d)
    for l in range(depth):
        qlat, kvlat, dnqkv, dnab, dil, z, gates = _in_projection(x2, norm_g[l], w_in_p[l])
        q, k, vt = _mla_prep(qlat, kvlat, cr, sa, sb, mla_q_a_norm_g[l].reshape(1, -1), w_q_p[l],
                             mla_kv_a_norm_g[l].reshape(1, -1), w_k_p[l], w_vt_p[l], gq_p[l], gk_p[l], tq)
        y_a = _mla_attention(q.reshape(b, s, -1), k.reshape(b, s, -1),
                             vt.reshape(b, s // tq, MLA_HEADS * MLA_VT_ROWS, tq), tq)
        y_b = _deltanet(dnqkv.reshape(b, s, -1), dnab.reshape(b, s, -1), dn_conv_w[l], dn_a_log[l], dn_dt_bias[l],
                        dn_out_norm_g[l])
        y_c = _dilated_attention(dil.reshape(b, s, -1), cos_h3, sin_h3, dil_q_norm_g[l], dil_k_norm_g[l])
        x2 = _merge(x2, y_a.reshape(t, -1), y_b.reshape(t, -1), y_c.reshape(t, -1), z, gates, w_branch_b[l],
                    w_out_b[l])
    return x2.reshape(b, s, d)
```

```python
import functools

import jax
import jax.numpy as jnp
from jax import lax
from jax.experimental import pallas as pl
from jax.experimental.pallas import tpu as pltpu

F32 = jnp.float32
BF16 = jnp.bfloat16

RMS_EPS = 1e-6
ROPE_THETA = 10000.0
LANE = 128

MLA_HEADS = 4
MLA_NOPE = 128
MLA_ROPE = 64
MLA_V = 128
MLA_QK = MLA_NOPE + MLA_ROPE
MLA_Q_RANK = 384
MLA_KV_RANK = 256
MLA_KV_PAD = 384
MLA_HEAD_PAD = 256
MLA_VT_ROWS = MLA_V + 16
LOG2_E = 1.4426950408889634
MLA_KEY_BLOCK = 512
MLA_QUERY_TILE = 512

DN_HEADS = 4
DN_HD = 128
DN_WIDTH = DN_HEADS * DN_HD
DN_CONV = 4
DN_CHUNK = 64
DN_HALO = 16
DN_TILE = 1024
DN_PREP_UNROLL = 4

DIL_WINDOWS = (128, 512, 2048)
DIL_DILATIONS = (1, 4, 16)
DIL_GROUPS = 3
DIL_HPG = 4
DIL_HD = 128
DIL_BLOCK = 128
DIL_QKV_WIDTH = DIL_GROUPS * DIL_HPG * DIL_HD
DIL_UNROLL = 4
DIL_ORDER = (2, 0, 1)
DIL_RELAYOUT_STRIDE = 4

N_BRANCHES = 3
BRANCH_WIDTH = 512

W_QLAT = MLA_Q_RANK
W_KVLAT = MLA_KV_PAD
W_DNQKV = 3 * DN_WIDTH
W_DNAB = LANE
W_DIL = 3 * DIL_QKV_WIDTH
W_Z = N_BRANCHES * BRANCH_WIDTH

VMEM_LIMIT = 56 * 1024 * 1024


def _cparams(*sem):
    return pltpu.CompilerParams(dimension_semantics=sem, vmem_limit_bytes=VMEM_LIMIT)


def _nt(a, b):
    return lax.dot_general(a, b, (((1,), (1,)), ((), ())), preferred_element_type=F32)


def _dot(a, b):
    return jnp.dot(a, b, preferred_element_type=F32)


def _run_staged(gens, skew):
    results = [None] * len(gens)
    done = [False] * len(gens)
    t = 0
    while not all(done):
        for n, gen in enumerate(gens):
            if n * skew <= t and not done[n]:
                try:
                    next(gen)
                except StopIteration as stop:
                    results[n] = stop.value
                    done[n] = True
        t += 1
    return results


def _dot_split3(a, x):
    hi = x.astype(BF16)
    r1 = x - hi.astype(F32)
    mid = r1.astype(BF16)
    lo = (r1 - mid.astype(F32)).astype(BF16)
    return _dot(a, hi) + _dot(a, mid) + _dot(a, lo)


def _rope_kernel(pos_ref, c_ref, cosh_ref, sinh_ref, cr_ref, sa_ref, sb_ref):
    pos = pos_ref[...].astype(F32)
    c = c_ref[...]
    ang_h = pos * c[0:1, :]
    cosh_ref[...] = jnp.cos(ang_h)
    sinh_ref[...] = jnp.sin(ang_h) * c[1:2, :]
    ang_r = pos * c[2:3, :]
    sr = jnp.sin(ang_r)
    cr_ref[...] = jnp.cos(ang_r) * c[3:4, :]
    sa_ref[...] = sr * c[4:5, :]
    sb_ref[...] = sr * c[5:6, :]


def _rope_tables(positions):
    t = positions.size
    ts = min(t, 1024)
    half_h = DIL_HD // 2
    half_r = MLA_ROPE // 2
    inv_h = 1.0 / (ROPE_THETA ** (jnp.arange(0, DIL_HD, 2, dtype=F32) / DIL_HD))
    inv_r = 1.0 / (ROPE_THETA ** (jnp.arange(0, MLA_ROPE, 2, dtype=F32) / MLA_ROPE))
    zeros = jnp.zeros
    ones = jnp.ones
    rows = [
        jnp.concatenate([inv_h, inv_h]),
        jnp.concatenate([-ones(half_h, F32), ones(half_h, F32)]),
        jnp.concatenate([inv_r, inv_r, zeros(LANE - MLA_ROPE, F32)]),
        jnp.concatenate([ones(MLA_ROPE, F32), zeros(LANE - MLA_ROPE, F32)]),
        jnp.concatenate([-ones(half_r, F32), zeros(LANE - half_r, F32)]),
        jnp.concatenate([zeros(half_r, F32), ones(half_r, F32), zeros(LANE - MLA_ROPE, F32)]),
        zeros(LANE, F32),
        zeros(LANE, F32),
    ]
    consts = jnp.stack(rows)
    tab = jax.ShapeDtypeStruct((t, LANE), F32)
    spec = pl.BlockSpec((ts, LANE), lambda i: (i, 0))
    return pl.pallas_call(
        _rope_kernel,
        grid=(t // ts,),
        in_specs=[pl.BlockSpec((ts, 1), lambda i: (i, 0)), pl.BlockSpec((8, LANE), lambda i: (0, 0))],
        out_specs=[spec] * 5,
        out_shape=[tab] * 5,
        compiler_params=_cparams("parallel"),
        name="rope_tables",
    )(positions.reshape(t, 1), consts)


_INPROJ_COLS = 512


def _inproj_kernel(x_ref, g_ref, w_ref, *out_refs):
    x = x_ref[...]
    h = (x * lax.rsqrt(jnp.mean(x * x, axis=-1, keepdims=True) + RMS_EPS) * g_ref[...]).astype(BF16)
    off = 0
    for o_ref in out_refs:
        width = o_ref.shape[-1]
        for c0 in range(0, width, _INPROJ_COLS):
            cw = min(_INPROJ_COLS, width - c0)
            acc = _dot(h, w_ref[:, off + c0:off + c0 + cw])
            o_ref[:, c0:c0 + cw] = acc.astype(o_ref.dtype)
        off += width


def _in_projection(x2, norm_g, w_in_p):
    t, d = x2.shape
    tm = min(t, 256)
    widths = (W_QLAT, W_KVLAT, W_DNQKV, W_DNAB, W_DIL, W_Z, N_BRANCHES * d)
    dtypes = (BF16, BF16, BF16, F32, BF16, BF16, BF16)
    assert sum(widths) == w_in_p.shape[1]
    return pl.pallas_call(
        _inproj_kernel,
        grid=(t // tm,),
        in_specs=[
            pl.BlockSpec((tm, d), lambda i: (i, 0)),
            pl.BlockSpec((1, d), lambda i: (0, 0)),
            pl.BlockSpec(memory_space=pltpu.VMEM),
        ],
        out_specs=[pl.BlockSpec((tm, w), lambda i: (i, 0)) for w in widths],
        out_shape=[jax.ShapeDtypeStruct((t, w), dt) for w, dt in zip(widths, dtypes)],
        compiler_params=_cparams("parallel"),
        name="in_projection",
    )(x2, norm_g.reshape(1, d), w_in_p)


def _mla_prep_kernel(qlat_ref, kvlat_ref, cr_ref, sa_ref, sb_ref, gqa_ref, wq_ref, gkva_ref, wk_ref, wvt_ref,
                     gq_ref, gk_ref, q_ref, k_ref, vt_ref):
    def rms(x, gain, n):
        return x * lax.rsqrt(jnp.sum(x * x, axis=-1, keepdims=True) * (1.0 / n) + RMS_EPS) * gain

    cr, sa, sb = cr_ref[...], sa_ref[...], sb_ref[...]

    def rope(x):
        return x * cr + pltpu.roll(x, LANE - MLA_ROPE // 2, 1) * sa + pltpu.roll(x, MLA_ROPE // 2, 1) * sb

    scale = MLA_QK ** -0.5 * LOG2_E
    qn = rms(qlat_ref[...].astype(F32), gqa_ref[...], MLA_Q_RANK).astype(BF16)
    q = _dot(qn, wq_ref[...])
    kvl = kvlat_ref[...].astype(F32)
    cn = rms(kvl[:, :MLA_KV_RANK], gkva_ref[...], MLA_KV_RANK).astype(BF16)
    kn = _dot(cn, wk_ref[...])
    vt = _nt(wvt_ref[...], cn).astype(BF16)
    for h in range(MLA_HEADS):
        r0 = MLA_VT_ROWS * h
        vt_ref[0, r0:r0 + MLA_V, :] = vt[MLA_V * h:MLA_V * (h + 1)]
        vt_ref[0, r0 + MLA_V:r0 + MLA_VT_ROWS, :] = jnp.ones((MLA_VT_ROWS - MLA_V, vt.shape[1]), BF16)
    k_pe = rope(rms(kvl[:, MLA_KV_RANK:], gk_ref[1:2, :], MLA_ROPE)).astype(BF16)
    for h in range(MLA_HEADS):
        c0 = MLA_HEAD_PAD * h
        q_nope = rms(q[:, c0:c0 + MLA_NOPE], gq_ref[0:1, :], MLA_NOPE)
        q_pe = rope(rms(q[:, c0 + MLA_NOPE:c0 + MLA_HEAD_PAD], gq_ref[1:2, :], MLA_ROPE))
        q_ref[:, c0:c0 + MLA_NOPE] = (q_nope * scale).astype(BF16)
        q_ref[:, c0 + MLA_NOPE:c0 + MLA_HEAD_PAD] = (q_pe * scale).astype(BF16)
        k_nope = rms(kn[:, MLA_NOPE * h:MLA_NOPE * (h + 1)], gk_ref[0:1, :], MLA_NOPE)
        k_ref[:, c0:c0 + MLA_NOPE] = k_nope.astype(BF16)
        k_ref[:, c0 + MLA_NOPE:c0 + MLA_HEAD_PAD] = k_pe


def _mla_prep(qlat, kvlat, cr, sa, sb, gqa, wq, gkva, wk, wvt, gq, gk, tm):
    t = qlat.shape[0]
    row = lambda w: pl.BlockSpec((tm, w), lambda i: (i, 0))
    full = lambda a: pl.BlockSpec(a.shape, lambda i: (0,) * a.ndim)
    qk_w = MLA_HEADS * MLA_HEAD_PAD
    v_w = MLA_HEADS * MLA_VT_ROWS
    return pl.pallas_call(
        _mla_prep_kernel,
        grid=(t // tm,),
        in_specs=[row(W_QLAT), row(W_KVLAT), row(LANE), row(LANE), row(LANE),
                  full(gqa), full(wq), full(gkva), full(wk), full(wvt), full(gq), full(gk)],
        out_specs=[row(qk_w), row(qk_w), pl.BlockSpec((1, v_w, tm), lambda i: (i, 0, 0))],
        out_shape=[jax.ShapeDtypeStruct((t, qk_w), BF16), jax.ShapeDtypeStruct((t, qk_w), BF16),
                   jax.ShapeDtypeStruct((t // tm, v_w, tm), BF16)],
        compiler_params=_cparams("parallel"),
        name="mla_prep",
    )(qlat, kvlat, cr, sa, sb, gqa, wq, gkva, wk, wvt, gq, gk)


def _mla_attn_kernel(q_ref, k_ref, vt_ref, o_ref, m_ref, acc_ref, *, tq, tk):
    qi = pl.program_id(1)
    kpq = tq // tk
    m_ref[...] = jnp.full(m_ref.shape, -jnp.inf, F32)
    acc_ref[...] = jnp.zeros(acc_ref.shape, F32)

    def head_step(j, start, h, diagonal):
        cols = slice(MLA_HEAD_PAD * h, MLA_HEAD_PAD * (h + 1))
        st = _nt(k_ref[0, pl.ds(start, tk), cols], q_ref[0, :, cols])
        yield
        if diagonal:
            key = lax.broadcasted_iota(jnp.int32, st.shape, 0) + (j - qi * kpq) * tk
            qry = lax.broadcasted_iota(jnp.int32, st.shape, 1)
            st = jnp.where(key <= qry, st, -jnp.inf)
        m_old = m_ref[h]
        m_new = jnp.maximum(m_old, jnp.max(st, axis=0, keepdims=True))
        alpha = jnp.exp2(m_old - m_new)
        p = jnp.exp2(st - m_new).astype(BF16)
        yield
        vt = vt_ref[0, j, MLA_VT_ROWS * h:MLA_VT_ROWS * (h + 1), :]
        acc_ref[h] = alpha * acc_ref[h] + _dot(vt, p)
        m_ref[h] = m_new

    def step(j, diagonal):
        start = pl.multiple_of(j * tk, tk)
        _run_staged([head_step(j, start, h, diagonal) for h in range(MLA_HEADS)], skew=1)

    def body(j, carry):
        step(j, False)
        return carry

    lax.fori_loop(0, qi * kpq, body, 0)
    for d in range(kpq):
        step(qi * kpq + d, True)
    for h in range(MLA_HEADS):
        o = acc_ref[h, :MLA_V, :] / acc_ref[h, MLA_V:MLA_V + 1, :]
        o_ref[0, :, MLA_V * h:MLA_V * (h + 1)] = o.T.astype(o_ref.dtype)


def _mla_attention(q, k, vt, tq, tk):
    b, s, qk_w = q.shape
    v_w = MLA_HEADS * MLA_V
    vt_w = MLA_HEADS * MLA_VT_ROWS
    assert tq % tk == 0 and s % tq == 0
    return pl.pallas_call(
        functools.partial(_mla_attn_kernel, tq=tq, tk=tk),
        grid=(b, s // tq),
        in_specs=[
            pl.BlockSpec((1, tq, qk_w), lambda bi, i: (bi, i, 0)),
            pl.BlockSpec((1, s, qk_w), lambda bi, i: (bi, 0, 0)),
            pl.BlockSpec((1, s // tk, vt_w, tk), lambda bi, i: (bi, 0, 0, 0)),
        ],
        out_specs=pl.BlockSpec((1, tq, v_w), lambda bi, i: (bi, i, 0)),
        out_shape=jax.ShapeDtypeStruct((b, s, v_w), BF16),
        scratch_shapes=[pltpu.VMEM((MLA_HEADS, 1, tq), F32), pltpu.VMEM((MLA_HEADS, MLA_VT_ROWS, tq), F32)],
        compiler_params=_cparams("parallel", "arbitrary"),
        name="mla_attention",
    )(q, k, vt)


def _block_diag(x, nblk, bw):
    blk = lax.broadcasted_iota(jnp.int32, x.shape, 1) // bw
    return jnp.concatenate([jnp.where(blk == h, x, 0.0) for h in range(nblk)], axis=0)


def _dn_kernel(x_ref, halo_ref, ab_ref, cw_ref, prm_ref, og_ref, y_ref,
               s_ref, xs_ref, q_s, k_s, v_s, g_s, b_s, cg_s, bm_s, hm_s, egl_s, *, tc):
    i = pl.program_id(1)
    nh, hd, ck = DN_HEADS, DN_HD, DN_CHUNK

    @pl.when(i == 0)
    def _():
        s_ref[...] = jnp.zeros(s_ref.shape, F32)

    xs_ref[0:DN_HALO, :] = jnp.where(i > 0, halo_ref[0].astype(F32), 0.0)
    xs_ref[DN_HALO:DN_HALO + tc, :] = x_ref[0].astype(F32)
    neg_a = -jnp.exp(prm_ref[0:1, :])
    lane = lax.broadcasted_iota(jnp.int32, (ck, LANE), 1)

    def mix_chunk(c):
        r0 = c * ck
        for cb in range(3 * nh):
            cols = slice(hd * cb, hd * (cb + 1))
            y = None
            for j in range(DN_CONV):
                t0 = DN_HALO + r0 - (DN_CONV - 1) + j
                tap = xs_ref[t0:t0 + ck, cols] * cw_ref[j:j + 1, cols]
                y = tap if y is None else y + tap
            y = y * jax.nn.sigmoid(y)
            if cb < 2 * nh:
                y = y * lax.rsqrt(jnp.sum(y * y, axis=-1, keepdims=True) + 1e-6)
            if cb < nh:
                q_s[r0:r0 + ck, cols] = y * (hd ** -0.5)
            elif cb < 2 * nh:
                k_s[r0:r0 + ck, hd * (cb - nh):hd * (cb - nh + 1)] = y
            else:
                v_s[r0:r0 + ck, hd * (cb - 2 * nh):hd * (cb - 2 * nh + 1)] = y
            yield
        ab = ab_ref[0, r0:r0 + ck, :]
        xg = ab + prm_ref[1:2, :]
        softplus = jnp.maximum(xg, 0.0) + jnp.log1p(jnp.exp(-jnp.abs(xg)))
        g_s[r0:r0 + ck, :] = jnp.where(lane < nh, neg_a * softplus, 0.0)
        b_s[r0:r0 + ck, :] = jax.nn.sigmoid(ab)

    cat = nh * ck
    ri = lax.broadcasted_iota(jnp.int32, (ck, ck), 0)
    ci = lax.broadcasted_iota(jnp.int32, (ck, ck), 1)
    tri = jnp.where(ri >= ci, 1.0, 0.0).astype(BF16)
    ones = jnp.ones((ck, ck), BF16)
    ii = lax.broadcasted_iota(jnp.int32, (ck, cat), 0)
    lane_c = lax.broadcasted_iota(jnp.int32, (ck, cat), 1)
    jj = lane_c % ck
    blk_c = lane_c // ck
    blk_k = lax.broadcasted_iota(jnp.int32, (ck, nh * hd), 1) // hd
    og = og_ref[...]

    def prep_chunk(c):
        rows = slice(c * ck, (c + 1) * ck)
        qc, kc, vc = q_s[rows, :], k_s[rows, :], v_s[rows, :]
        gc = _dot_split3(tri, g_s[rows, :])
        yield
        bch = b_s[rows, :]
        glast = gc[ck - 1:ck, :]

        def cols_to_heads(a, lane0, width):
            return jnp.concatenate(
                [jnp.broadcast_to(a[:, lane0 + h:lane0 + h + 1], (a.shape[0], width)) for h in range(nh)], axis=1)

        gc_w = cols_to_heads(gc, 0, hd)
        beta_w = cols_to_heads(bch, nh, hd)
        glast_w = cols_to_heads(glast, 0, hd)
        eg = jnp.exp(gc_w)
        kb = kc * beta_w
        vb = vc * beta_w
        qg = qc * eg
        kbg = kb * eg
        kd = kc * jnp.exp(glast_w - gc_w)

        colcat = jnp.zeros((ck, cat), F32)
        for h in range(nh):
            colcat = jnp.where(blk_c == h, jnp.broadcast_to(gc[:, h:h + 1], (ck, cat)), colcat)
        rowcat = _dot_split3(ones, jnp.where(ii == jj, colcat, 0.0))
        yield
        diff = colcat - rowcat
        d_inc = jnp.exp(jnp.where(ii >= jj, diff, -jnp.inf))
        d_str = jnp.where(ii > jj, d_inc, 0.0)

        lhs = jnp.concatenate([kb, qc], axis=0).astype(BF16)
        bdk = jnp.concatenate([jnp.where(blk_k == h, kc, 0.0) for h in range(nh)], axis=0).astype(BF16)
        aq = _nt(lhs, bdk)
        yield
        low = aq[:ck] * d_str
        qk = aq[ck:] * d_inc

        m = -low
        p = low
        bdp = _block_diag(p, nh, ck).astype(BF16)
        n_sq = ck.bit_length() - 2
        for _ in range(n_sq):
            p = _dot(p.astype(BF16), bdp)
            yield
            bdp = _block_diag(p, nh, ck).astype(BF16)
            m = m + p + _dot(m.astype(BF16), bdp)
        yield

        rhs = jnp.concatenate(
            [jnp.concatenate([vb[:, hd * h:hd * (h + 1)], kbg[:, hd * h:hd * (h + 1)]], axis=1) for h in range(nh)],
            axis=0)
        sol = rhs + _dot(_block_diag(m, nh, ck).astype(BF16), rhs.astype(BF16))
        yield
        sol_b = sol.astype(BF16)
        qkuw = _dot(_block_diag(qk, nh, ck).astype(BF16), sol_b)
        cg, bmat = [], []
        for h in range(nh):
            hs = slice(ck * h, ck * (h + 1))
            bc = lax.dot_general(kd[:, hd * h:hd * (h + 1)].astype(BF16), sol_b[hs], (((0,), (0,)), ((), ())),
                                 preferred_element_type=F32)
            bmat.append(bc[:, :hd])
            gmat = qg[:, hd * h:hd * (h + 1)] - qkuw[hs, hd:]
            cg.append(jnp.concatenate([bc[:, hd:], gmat], axis=0).astype(BF16))
        return cg, bmat, qkuw[:, :hd], jnp.broadcast_to(jnp.exp(glast), (8, LANE))

    def recur_chunks(chunks):
        for c in chunks:
            rows = slice(c * ck, (c + 1) * ck)
            egl = egl_s[c]
            rs = [_dot(cg_s[c, h], s_ref[h].astype(BF16)) for h in range(nh)]
            yield
            for h in range(nh):
                s_ref[h] = s_ref[h] * egl[0:1, h:h + 1] - rs[h][:hd] + bm_s[c, h]
                o = rs[h][hd:] + hm_s[c, ck * h:ck * (h + 1), :]
                o = o * lax.rsqrt(jnp.mean(o * o, axis=-1, keepdims=True) + RMS_EPS) * og
                y_ref[0, rows, hd * h:hd * (h + 1)] = o.astype(y_ref.dtype)
            yield

    unroll = DN_PREP_UNROLL
    groups = [list(range(g0, g0 + unroll)) for g0 in range(0, tc // ck, unroll)]
    for step in range(len(groups) + 2):
        gens = []
        if step - 1 in range(len(groups)):
            prep_ids = groups[step - 1]
            gens += [prep_chunk(c) for c in prep_ids]
        if step in range(len(groups)):
            gens += [mix_chunk(c) for c in groups[step]]
        if step - 2 in range(len(groups)):
            gens.append(recur_chunks(groups[step - 2]))
        results = _run_staged(gens, skew=0)
        if step - 1 in range(len(groups)):
            for c, (cgm, bmat, hmat, egl) in zip(prep_ids, results):
                for h in range(nh):
                    cg_s[c, h] = cgm[h]
                    bm_s[c, h] = bmat[h]
                hm_s[c] = hmat
                egl_s[c] = egl


def _deltanet(dnqkv, dnab, conv_w, a_log, dt_bias, out_norm_g):
    b, s, _ = dnqkv.shape
    tc = min(s, DN_TILE)
    hb = tc // DN_HALO
    nck = tc // DN_CHUNK
    assert nck % DN_PREP_UNROLL == 0
    prm = jnp.zeros((8, LANE), F32).at[0, :DN_HEADS].set(a_log).at[1, :DN_HEADS].set(dt_bias)
    return pl.pallas_call(
        functools.partial(_dn_kernel, tc=tc),
        grid=(b, s // tc),
        in_specs=[
            pl.BlockSpec((1, tc, W_DNQKV), lambda bi, i: (bi, i, 0)),
            pl.BlockSpec((1, DN_HALO, W_DNQKV), lambda bi, i: (bi, jnp.maximum(i * hb - 1, 0), 0)),
            pl.BlockSpec((1, tc, LANE), lambda bi, i: (bi, i, 0)),
            pl.BlockSpec((DN_CONV, W_DNQKV), lambda bi, i: (0, 0)),
            pl.BlockSpec((8, LANE), lambda bi, i: (0, 0)),
            pl.BlockSpec((1, DN_HD), lambda bi, i: (0, 0)),
        ],
        out_specs=pl.BlockSpec((1, tc, DN_WIDTH), lambda bi, i: (bi, i, 0)),
        out_shape=jax.ShapeDtypeStruct((b, s, DN_WIDTH), BF16),
        scratch_shapes=[
            pltpu.VMEM((DN_HEADS, DN_HD, DN_HD), F32),
            pltpu.VMEM((DN_HALO + tc, W_DNQKV), F32),
            pltpu.VMEM((tc, DN_WIDTH), F32), pltpu.VMEM((tc, DN_WIDTH), F32), pltpu.VMEM((tc, DN_WIDTH), F32),
            pltpu.VMEM((tc, LANE), F32), pltpu.VMEM((tc, LANE), F32),
            pltpu.VMEM((nck, DN_HEADS, DN_HD + DN_CHUNK, DN_HD), BF16),
            pltpu.VMEM((nck, DN_HEADS, DN_HD, DN_HD), F32),
            pltpu.VMEM((nck, DN_HEADS * DN_CHUNK, DN_HD), F32),
            pltpu.VMEM((nck, 8, LANE), F32),
        ],
        compiler_params=_cparams("parallel", "arbitrary"),
        name="gated_deltanet",
    )(dnqkv, dnqkv, dnab, conv_w, prm, out_norm_g.reshape(1, DN_HD))


_DIL_PREP_ROWS = 512


def _dil_kernel(q_ref, k_ref, v_ref, cos_ref, sin_ref, gq_ref, gk_ref, o_ref,
                qf, kf, vf, m_run, l_run, acc, tmp, *, seq):
    g = pl.program_id(2)
    blk = DIL_BLOCK
    rt = min(seq, _DIL_PREP_ROWS)
    ones_sq = jnp.ones((DIL_HD, DIL_HD), BF16)

    def prep(t, carry):
        rows = pl.ds(pl.multiple_of(t * rt, rt), rt)
        cs, sn = cos_ref[0, rows, :], sin_ref[0, rows, :]

        def norm_rope(x, gain):
            ms = _dot((x * x).astype(BF16), ones_sq) * (1.0 / DIL_HD)
            x = x * lax.rsqrt(ms + RMS_EPS) * gain
            return x * cs + pltpu.roll(x, DIL_HD // 2, 1) * sn

        qf[rows, :] = norm_rope(q_ref[0, rows, :].astype(F32), gq_ref[...]) * (DIL_HD ** -0.5 * LOG2_E)
        kf[rows, :] = norm_rope(k_ref[0, rows, :].astype(F32), gk_ref[...])
        vf[rows, :] = v_ref[0, rows, :].astype(F32)
        return carry

    lax.fori_loop(0, seq // rt, prep, 0)

    qi = lax.broadcasted_iota(jnp.int32, (blk, 2 * blk), 0)
    kj = lax.broadcasted_iota(jnp.int32, (blk, 2 * blk), 1)
    band = jnp.logical_and(kj >= qi, kj <= qi + blk)
    prev_half = kj < blk

    fct = DIL_RELAYOUT_STRIDE
    quarter = seq // fct
    cls_len = seq // (fct * fct)
    rl = min(cls_len, 256)

    def class_major(src):
        for r1 in range(fct):
            for c0 in range(0, quarter, rl):
                tmp[r1 * quarter + c0:r1 * quarter + c0 + rl, :] = src[pl.ds(r1 + c0 * fct, rl, stride=fct), :]
        for r1 in range(fct):
            for r2 in range(fct):
                for c0 in range(0, cls_len, rl):
                    d0 = (r1 * fct + r2) * cls_len + c0
                    src[d0:d0 + rl, :] = tmp[pl.ds(r1 * quarter + r2 + c0 * fct, rl, stride=fct), :]

    def token_major(src):
        for r1 in range(fct):
            for r2 in range(fct):
                for c0 in range(0, cls_len, rl):
                    d0 = (r1 * fct + r2) * cls_len + c0
                    tmp[pl.ds(r1 * quarter + r2 + c0 * fct, rl, stride=fct), :] = src[d0:d0 + rl, :]
        for r1 in range(fct):
            for c0 in range(0, quarter, rl):
                src[pl.ds(r1 + c0 * fct, rl, stride=fct), :] = tmp[r1 * quarter + c0:r1 * quarter + c0 + rl, :]

    def group(dil, first, relayout):
        nb = seq // (dil * blk)
        if relayout:
            assert first and dil == fct * fct
            for ref in (qf, kf, vf):
                class_major(ref)

        def rows_at(start):
            return pl.ds(start, blk) if (dil == 1 or relayout) else pl.ds(start, blk, stride=dil)

        def block(idx):
            r = idx // nb
            n = idx % nb
            if relayout:
                start = pl.multiple_of(idx * blk, blk)
                pstart = pl.multiple_of(jnp.maximum(idx - 1, 0) * blk, blk)
            else:
                start = n * (blk * dil) + r
                pstart = jnp.maximum(start - blk * dil, r)
            rows = rows_at(start)
            prow = rows_at(pstart)
            qb = qf[rows, :].astype(BF16)
            kcat = jnp.concatenate([kf[prow, :], kf[rows, :]], axis=0).astype(BF16)
            vcat = jnp.concatenate([vf[prow, :], vf[rows, :]], axis=0).astype(BF16)
            s = _nt(qb, kcat)
            yield
            no_prev = jnp.where(n > 0, 0.0, -jnp.inf)
            s = jnp.where(band, s + jnp.where(prev_half, no_prev, 0.0), -jnp.inf)
            m_b = jnp.max(s, axis=-1, keepdims=True)
            yield
            p = jnp.exp2(s - m_b)
            l_b = jnp.sum(p, axis=-1, keepdims=True)
            pv = _dot(p.astype(BF16), vcat)
            yield
            if first:
                return rows, jnp.broadcast_to(m_b, (blk, DIL_HD)), jnp.broadcast_to(l_b, (blk, DIL_HD)), pv
            m_old = m_run[rows, :]
            m_new = jnp.maximum(m_old, m_b)
            a_old = jnp.exp2(m_old - m_new)
            a_b = jnp.exp2(m_b - m_new)
            return rows, m_new, l_run[rows, :] * a_old + l_b * a_b, acc[rows, :] * a_old + pv * a_b

        def block_group(ig, carry):
            results = _run_staged([block(ig * DIL_UNROLL + un) for un in range(DIL_UNROLL)], skew=0)
            for rows, m_new, l_new, acc_new in results:
                m_run[rows, :] = m_new
                l_run[rows, :] = l_new
                acc[rows, :] = acc_new
            return carry

        lax.fori_loop(0, seq // (blk * DIL_UNROLL), block_group, 0)
        if relayout:
            for ref in (m_run, l_run, acc):
                token_major(ref)

    for j, gi in enumerate(DIL_ORDER):
        dil = DIL_DILATIONS[gi]
        pl.when(g == j)(functools.partial(group, dil, j == 0, dil == DIL_RELAYOUT_STRIDE ** 2))

    @pl.when(g == DIL_GROUPS - 1)
    def _():
        def fin(t, carry):
            rows = pl.ds(pl.multiple_of(t * rt, rt), rt)
            o_ref[0, rows, :] = (acc[rows, :] / l_run[rows, :]).astype(o_ref.dtype)
            return carry

        lax.fori_loop(0, seq // rt, fin, 0)


def _dilated_attention(dil, cos_h, sin_h, gq, gk):
    b, s, _ = dil.shape
    nheads = DIL_GROUPS * DIL_HPG
    assert all(w // d == DIL_BLOCK for w, d in zip(DIL_WINDOWS, DIL_DILATIONS))
    assert s % (max(DIL_DILATIONS) * DIL_BLOCK) == 0
    assert DIL_ORDER == tuple((j + DIL_ORDER[0]) % DIL_GROUPS for j in range(DIL_GROUPS))
    grp = lambda j: (j + DIL_ORDER[0]) % DIL_GROUPS
    part = lambda p: pl.BlockSpec((1, s, DIL_HD), lambda bi, h, j: (bi, 0, p * nheads + grp(j) * DIL_HPG + h))
    tab = pl.BlockSpec((1, s, DIL_HD), lambda bi, h, g: (bi, 0, 0))
    gain = pl.BlockSpec((1, DIL_HD), lambda bi, h, g: (0, 0))
    scr = pltpu.VMEM((s, DIL_HD), F32)
    return pl.pallas_call(
        functools.partial(_dil_kernel, seq=s),
        grid=(b, DIL_HPG, DIL_GROUPS),
        in_specs=[part(0), part(1), part(2), tab, tab, gain, gain],
        out_specs=pl.BlockSpec((1, s, DIL_HD), lambda bi, h, g: (bi, 0, h)),
        out_shape=jax.ShapeDtypeStruct((b, s, DIL_HPG * DIL_HD), BF16),
        scratch_shapes=[scr] * 7,
        compiler_params=_cparams("parallel", "parallel", "arbitrary"),
        name="dilated_attention",
    )(dil, dil, dil, cos_h, sin_h, gq.reshape(1, DIL_HD), gk.reshape(1, DIL_HD))


def _merge_kernel(x_ref, ya_ref, yb_ref, yc_ref, z_ref, gate_ref, wb_ref, wo_ref, o_ref):
    d = x_ref.shape[-1]
    mixed = None
    for n, y_ref in enumerate((ya_ref, yb_ref, yc_ref)):
        z = z_ref[:, BRANCH_WIDTH * n:BRANCH_WIDTH * (n + 1)].astype(F32)
        ys = (y_ref[...].astype(F32) * (z * jax.nn.sigmoid(z))).astype(BF16)
        branch = _dot(ys, wb_ref[n])
        gate = jax.nn.sigmoid(gate_ref[:, d * n:d * (n + 1)].astype(F32))
        mixed = gate * branch if mixed is None else mixed + gate * branch
    o_ref[...] = x_ref[...] + _dot(mixed.astype(BF16), wo_ref[...])


def _merge(x2, ya, yb, yc, z, gates, w_branch, w_out):
    t, d = x2.shape
    tm = min(t, 512)
    row = lambda w: pl.BlockSpec((tm, w), lambda i: (i, 0))
    return pl.pallas_call(
        _merge_kernel,
        grid=(t // tm,),
        in_specs=[row(d), row(BRANCH_WIDTH), row(BRANCH_WIDTH), row(BRANCH_WIDTH), row(W_Z), row(N_BRANCHES * d),
                  pl.BlockSpec(w_branch.shape, lambda i: (0, 0, 0)), pl.BlockSpec(w_out.shape, lambda i: (0, 0))],
        out_specs=row(d),
        out_shape=jax.ShapeDtypeStruct((t, d), F32),
        compiler_params=_cparams("parallel"),
        name="merge_out",
    )(x2, ya, yb, yc, z, gates, w_branch, w_out)


def _prep_w_in(w_in):
    d = w_in.shape[-2]
    lead = w_in.shape[:-1]
    o_q = 0
    o_kv = o_q + MLA_Q_RANK
    o_za = o_kv + MLA_KV_RANK + MLA_ROPE
    o_dn = o_za + BRANCH_WIDTH
    o_a = o_dn + 3 * DN_WIDTH
    o_zb = o_a + 2 * DN_HEADS
    o_dil = o_zb + BRANCH_WIDTH
    o_zc = o_dil + 3 * DIL_QKV_WIDTH
    o_g = o_zc + BRANCH_WIDTH
    end = o_g + N_BRANCHES * d
    assert end == w_in.shape[-1]
    zeros = lambda n: jnp.zeros(lead + (n,), w_in.dtype)
    segs = [
        w_in[..., o_q:o_kv],
        w_in[..., o_kv:o_za], zeros(W_KVLAT - (o_za - o_kv)),
        w_in[..., o_dn:o_a],
        w_in[..., o_a:o_zb], zeros(W_DNAB - 2 * DN_HEADS),
        w_in[..., o_dil:o_zc],
        w_in[..., o_za:o_dn], w_in[..., o_zb:o_dil], w_in[..., o_zc:o_g],
        w_in[..., o_g:end],
    ]
    return jnp.concatenate(segs, axis=-1).astype(BF16)


def _prep_w_q_b(w):
    lead = w.shape[:-1]
    w = w.reshape(lead + (MLA_HEADS, MLA_QK))
    pad = jnp.zeros(lead + (MLA_HEADS, MLA_HEAD_PAD - MLA_QK), w.dtype)
    return jnp.concatenate([w, pad], axis=-1).reshape(lead + (MLA_HEADS * MLA_HEAD_PAD,)).astype(BF16)


def _prep_w_kv_b(w):
    lead = w.shape[:-1]
    w = w.reshape(lead + (MLA_HEADS, MLA_NOPE + MLA_V))
    k = w[..., :MLA_NOPE].reshape(lead + (MLA_HEADS * MLA_NOPE,))
    v = w[..., MLA_NOPE:].reshape(lead + (MLA_HEADS * MLA_V,))
    return k.astype(BF16), jnp.swapaxes(v, -1, -2).astype(BF16)


def _prep_qk_gain(g):
    pad = jnp.zeros(g.shape[:-1] + (LANE - MLA_ROPE,), g.dtype)
    return jnp.stack([g[..., :MLA_NOPE], jnp.concatenate([g[..., MLA_NOPE:], pad], axis=-1)], axis=-2)


def kernel(x, positions, norm_g, w_in, mla_q_a_norm_g, mla_w_q_b, mla_kv_a_norm_g, mla_w_kv_b, mla_q_norm_g,
           mla_k_norm_g, dn_conv_w, dn_a_log, dn_dt_bias, dn_out_norm_g, dil_q_norm_g, dil_k_norm_g, w_branch,
           w_out):
    b, s, d = x.shape
    t = b * s
    depth = w_in.shape[0]

    cos_h, sin_h, cr, sa, sb = _rope_tables(positions)
    cos_h3 = cos_h.reshape(b, s, LANE)
    sin_h3 = sin_h.reshape(b, s, LANE)

    w_in_p = _prep_w_in(w_in)
    w_q_p = _prep_w_q_b(mla_w_q_b)
    w_k_p, w_vt_p = _prep_w_kv_b(mla_w_kv_b)
    tk = min(s, MLA_KEY_BLOCK)
    tq = min(s, MLA_QUERY_TILE)
    gq_p = _prep_qk_gain(mla_q_norm_g)
    gk_p = _prep_qk_gain(mla_k_norm_g)
    w_branch_b = w_branch.astype(BF16)
    w_out_b = w_out.astype(BF16)

    x2 = x.reshape(t, d)
    for l in range(depth):
        qlat, kvlat, dnqkv, dnab, dil, z, gates = _in_projection(x2, norm_g[l], w_in_p[l])
        q, k, vt = _mla_prep(qlat, kvlat, cr, sa, sb, mla_q_a_norm_g[l].reshape(1, -1), w_q_p[l],
                             mla_kv_a_norm_g[l].reshape(1, -1), w_k_p[l], w_vt_p[l], gq_p[l], gk_p[l], tk)
        y_a = _mla_attention(q.reshape(b, s, -1), k.reshape(b, s, -1),
                             vt.reshape(b, s // tk, MLA_HEADS * MLA_VT_ROWS, tk), tq, tk)
        y_b = _deltanet(dnqkv.reshape(b, s, -1), dnab.reshape(b, s, -1), dn_conv_w[l], dn_a_log[l], dn_dt_bias[l],
                        dn_out_norm_g[l])
        y_c = _dilated_attention(dil.reshape(b, s, -1), cos_h3, sin_h3, dil_q_norm_g[l], dil_k_norm_g[l])
        x2 = _merge(x2, y_a.reshape(t, -1), y_b.reshape(t, -1), y_c.reshape(t, -1), z, gates, w_branch_b[l],
                    w_out_b[l])
    return x2.reshape(b, s, d)
```

```python
import functools

import jax
import jax.numpy as jnp
from jax import lax
from jax.experimental import pallas as pl
from jax.experimental.pallas import tpu as pltpu

F32 = jnp.float32
BF16 = jnp.bfloat16

RMS_EPS = 1e-6
ROPE_THETA = 10000.0
LANE = 128

MLA_HEADS = 4
MLA_NOPE = 128
MLA_ROPE = 64
MLA_V = 128
MLA_QK = MLA_NOPE + MLA_ROPE
MLA_Q_RANK = 384
MLA_KV_RANK = 256
MLA_KV_PAD = 384
MLA_HEAD_PAD = 256
MLA_VT_ROWS = MLA_V + 16
LOG2_E = 1.4426950408889634
MLA_KEY_BLOCK = 512
MLA_QUERY_TILE = 512

DN_HEADS = 4
DN_HD = 128
DN_WIDTH = DN_HEADS * DN_HD
DN_CONV = 4
DN_CHUNK = 64
DN_HALO = 16
DN_TILE = 1024
DN_PREP_UNROLL = 4

DIL_WINDOWS = (128, 512, 2048)
DIL_DILATIONS = (1, 4, 16)
DIL_GROUPS = 3
DIL_HPG = 4
DIL_HD = 128
DIL_BLOCK = 128
DIL_QKV_WIDTH = DIL_GROUPS * DIL_HPG * DIL_HD
DIL_UNROLL = 8
DIL_ORDER = (2, 0, 1)
DIL_RELAYOUT_STRIDE = 4

N_BRANCHES = 3
BRANCH_WIDTH = 512

W_QLAT = MLA_Q_RANK
W_KVLAT = MLA_KV_PAD
W_DNQKV = 3 * DN_WIDTH
W_DNAB = LANE
W_DIL = 3 * DIL_QKV_WIDTH
W_Z = N_BRANCHES * BRANCH_WIDTH

VMEM_LIMIT = 56 * 1024 * 1024


def _cparams(*sem):
    return pltpu.CompilerParams(dimension_semantics=sem, vmem_limit_bytes=VMEM_LIMIT)


def _nt(a, b):
    return lax.dot_general(a, b, (((1,), (1,)), ((), ())), preferred_element_type=F32)


def _dot(a, b):
    return jnp.dot(a, b, preferred_element_type=F32)


def _run_staged(gens, skew):
    results = [None] * len(gens)
    done = [False] * len(gens)
    t = 0
    while not all(done):
        for n, gen in enumerate(gens):
            if n * skew <= t and not done[n]:
                try:
                    next(gen)
                except StopIteration as stop:
                    results[n] = stop.value
                    done[n] = True
        t += 1
    return results


def _dot_split3(a, x):
    hi = x.astype(BF16)
    r1 = x - hi.astype(F32)
    mid = r1.astype(BF16)
    lo = (r1 - mid.astype(F32)).astype(BF16)
    return _dot(a, hi) + _dot(a, mid) + _dot(a, lo)


def _rope_kernel(pos_ref, f_ref, cosh_ref, sinh_ref, cr_ref, sr_ref):
    half_h, half_r = DIL_HD // 2, MLA_ROPE // 2
    ang = pos_ref[...].astype(F32) * f_ref[0:1, :]
    c, s = jnp.cos(ang), jnp.sin(ang)
    c_sw, s_sw = pltpu.roll(c, half_h, 1), pltpu.roll(s, half_h, 1)
    lane = lax.broadcasted_iota(jnp.int32, ang.shape, 1)
    low = lane < half_h
    cosh_ref[...] = jnp.where(low, c, c_sw)
    sinh_ref[...] = jnp.where(low, -s, s_sw)
    first = lane < half_r
    second = jnp.logical_and(lane >= half_h, lane < half_h + half_r)
    cr_ref[...] = jnp.where(first, c_sw, jnp.where(second, c, 0.0))
    sr_ref[...] = jnp.where(first, -s_sw, jnp.where(second, s, 0.0))


def _spread_rope(a):
    half = MLA_ROPE // 2
    z = jnp.zeros(a.shape[:-1] + (LANE // 2 - half,), a.dtype)
    return jnp.concatenate([a[..., :half], z, a[..., half:], z], axis=-1)


def _rope_tables(positions):
    t = positions.size
    ts = min(t, 1024)
    inv_h = 1.0 / (ROPE_THETA ** (jnp.arange(0, DIL_HD, 2, dtype=F32) / DIL_HD))
    inv_r = 1.0 / (ROPE_THETA ** (jnp.arange(0, MLA_ROPE, 2, dtype=F32) / MLA_ROPE))
    freqs = jnp.concatenate([inv_h, inv_r, jnp.zeros(LANE - inv_h.size - inv_r.size, F32)])
    tab = jax.ShapeDtypeStruct((t, LANE), F32)
    spec = pl.BlockSpec((ts, LANE), lambda i: (i, 0))
    return pl.pallas_call(
        _rope_kernel,
        grid=(t // ts,),
        in_specs=[pl.BlockSpec((ts, 1), lambda i: (i, 0)), pl.BlockSpec((8, LANE), lambda i: (0, 0))],
        out_specs=[spec] * 4,
        out_shape=[tab] * 4,
        compiler_params=_cparams("parallel"),
        name="rope_tables",
    )(positions.reshape(t, 1), jnp.broadcast_to(freqs, (8, LANE)))


_INPROJ_COLS = 512


def _inproj_kernel(x_ref, g_ref, w_ref, *out_refs):
    x = x_ref[...]
    h = (x * lax.rsqrt(jnp.mean(x * x, axis=-1, keepdims=True) + RMS_EPS) * g_ref[...]).astype(BF16)
    off = 0
    for o_ref in out_refs:
        width = o_ref.shape[-1]
        for c0 in range(0, width, _INPROJ_COLS):
            cw = min(_INPROJ_COLS, width - c0)
            acc = _dot(h, w_ref[:, off + c0:off + c0 + cw])
            o_ref[:, c0:c0 + cw] = acc.astype(o_ref.dtype)
        off += width


def _in_projection(x2, norm_g, w_in_p):
    t, d = x2.shape
    tm = min(t, 256)
    widths = (W_QLAT, W_KVLAT, W_DNQKV, W_DNAB, W_DIL, W_Z, N_BRANCHES * d)
    dtypes = (BF16, BF16, BF16, F32, BF16, BF16, BF16)
    assert sum(widths) == w_in_p.shape[1]
    return pl.pallas_call(
        _inproj_kernel,
        grid=(t // tm,),
        in_specs=[
            pl.BlockSpec((tm, d), lambda i: (i, 0)),
            pl.BlockSpec((1, d), lambda i: (0, 0)),
            pl.BlockSpec(memory_space=pltpu.VMEM),
        ],
        out_specs=[pl.BlockSpec((tm, w), lambda i: (i, 0)) for w in widths],
        out_shape=[jax.ShapeDtypeStruct((t, w), dt) for w, dt in zip(widths, dtypes)],
        compiler_params=_cparams("parallel"),
        name="in_projection",
    )(x2, norm_g.reshape(1, d), w_in_p)


def _mla_prep_kernel(qlat_ref, kvlat_ref, cr_ref, sr_ref, gqa_ref, wq_ref, gkva_ref, wk_ref, wvt_ref,
                     gq_ref, gk_ref, q_ref, k_ref, vt_ref):
    def rms(x, gain, n):
        return x * lax.rsqrt(jnp.sum(x * x, axis=-1, keepdims=True) * (1.0 / n) + RMS_EPS) * gain

    cr, sr = cr_ref[...], sr_ref[...]

    def rope(x):
        return x * cr + pltpu.roll(x, LANE // 2, 1) * sr

    scale = MLA_QK ** -0.5 * LOG2_E
    qn = rms(qlat_ref[...].astype(F32), gqa_ref[...], MLA_Q_RANK).astype(BF16)
    q = _dot(qn, wq_ref[...])
    kvl = kvlat_ref[...].astype(F32)
    cn = rms(kvl[:, :MLA_KV_RANK], gkva_ref[...], MLA_KV_RANK).astype(BF16)
    kn = _dot(cn, wk_ref[...])
    vt = _nt(wvt_ref[...], cn).astype(BF16)
    for h in range(MLA_HEADS):
        r0 = MLA_VT_ROWS * h
        vt_ref[0, r0:r0 + MLA_V, :] = vt[MLA_V * h:MLA_V * (h + 1)]
        vt_ref[0, r0 + MLA_V:r0 + MLA_VT_ROWS, :] = jnp.ones((MLA_VT_ROWS - MLA_V, vt.shape[1]), BF16)
    k_pe = rope(rms(kvl[:, MLA_KV_RANK:], gk_ref[1:2, :], MLA_ROPE)).astype(BF16)
    for h in range(MLA_HEADS):
        c0 = MLA_HEAD_PAD * h
        q_nope = rms(q[:, c0:c0 + MLA_NOPE], gq_ref[0:1, :], MLA_NOPE)
        q_pe = rope(rms(q[:, c0 + MLA_NOPE:c0 + MLA_HEAD_PAD], gq_ref[1:2, :], MLA_ROPE))
        q_ref[:, c0:c0 + MLA_NOPE] = (q_nope * scale).astype(BF16)
        q_ref[:, c0 + MLA_NOPE:c0 + MLA_HEAD_PAD] = (q_pe * scale).astype(BF16)
        k_nope = rms(kn[:, MLA_NOPE * h:MLA_NOPE * (h + 1)], gk_ref[0:1, :], MLA_NOPE)
        k_ref[:, c0:c0 + MLA_NOPE] = k_nope.astype(BF16)
        k_ref[:, c0 + MLA_NOPE:c0 + MLA_HEAD_PAD] = k_pe


def _mla_prep(qlat, kvlat, cr, sr, gqa, wq, gkva, wk, wvt, gq, gk, tm):
    t = qlat.shape[0]
    row = lambda w: pl.BlockSpec((tm, w), lambda i: (i, 0))
    full = lambda a: pl.BlockSpec(a.shape, lambda i: (0,) * a.ndim)
    qk_w = MLA_HEADS * MLA_HEAD_PAD
    v_w = MLA_HEADS * MLA_VT_ROWS
    return pl.pallas_call(
        _mla_prep_kernel,
        grid=(t // tm,),
        in_specs=[row(W_QLAT), row(W_KVLAT), row(LANE), row(LANE),
                  full(gqa), full(wq), full(gkva), full(wk), full(wvt), full(gq), full(gk)],
        out_specs=[row(qk_w), row(qk_w), pl.BlockSpec((1, v_w, tm), lambda i: (i, 0, 0))],
        out_shape=[jax.ShapeDtypeStruct((t, qk_w), BF16), jax.ShapeDtypeStruct((t, qk_w), BF16),
                   jax.ShapeDtypeStruct((t // tm, v_w, tm), BF16)],
        compiler_params=_cparams("parallel"),
        name="mla_prep",
    )(qlat, kvlat, cr, sr, gqa, wq, gkva, wk, wvt, gq, gk)


def _mla_attn_kernel(q_ref, k_ref, vt_ref, o_ref, m_ref, acc_ref, *, tq, tk):
    qi = pl.program_id(1)
    kpq = tq // tk
    m_ref[...] = jnp.full(m_ref.shape, -jnp.inf, F32)
    acc_ref[...] = jnp.zeros(acc_ref.shape, F32)

    def head_step(j, start, h, diagonal):
        cols = slice(MLA_HEAD_PAD * h, MLA_HEAD_PAD * (h + 1))
        st = _nt(k_ref[0, pl.ds(start, tk), cols], q_ref[0, :, cols])
        yield
        if diagonal:
            key = lax.broadcasted_iota(jnp.int32, st.shape, 0) + (j - qi * kpq) * tk
            qry = lax.broadcasted_iota(jnp.int32, st.shape, 1)
            st = jnp.where(key <= qry, st, -jnp.inf)
        m_old = m_ref[h]
        m_new = jnp.maximum(m_old, jnp.max(st, axis=0, keepdims=True))
        alpha = jnp.exp2(m_old - m_new)
        p = jnp.exp2(st - m_new).astype(BF16)
        yield
        vt = vt_ref[0, j, MLA_VT_ROWS * h:MLA_VT_ROWS * (h + 1), :]
        acc_ref[h] = alpha * acc_ref[h] + _dot(vt, p)
        m_ref[h] = m_new

    def step(j, diagonal):
        start = pl.multiple_of(j * tk, tk)
        _run_staged([head_step(j, start, h, diagonal) for h in range(MLA_HEADS)], skew=1)

    def body(j, carry):
        step(j, False)
        return carry

    lax.fori_loop(0, qi * kpq, body, 0)
    for d in range(kpq):
        step(qi * kpq + d, True)
    for h in range(MLA_HEADS):
        o = acc_ref[h, :MLA_V, :] / acc_ref[h, MLA_V:MLA_V + 1, :]
        o_ref[0, :, MLA_V * h:MLA_V * (h + 1)] = o.T.astype(o_ref.dtype)


def _mla_attention(q, k, vt, tq, tk):
    b, s, qk_w = q.shape
    v_w = MLA_HEADS * MLA_V
    vt_w = MLA_HEADS * MLA_VT_ROWS
    assert tq % tk == 0 and s % tq == 0
    return pl.pallas_call(
        functools.partial(_mla_attn_kernel, tq=tq, tk=tk),
        grid=(b, s // tq),
        in_specs=[
            pl.BlockSpec((1, tq, qk_w), lambda bi, i: (bi, i, 0)),
            pl.BlockSpec((1, s, qk_w), lambda bi, i: (bi, 0, 0)),
            pl.BlockSpec((1, s // tk, vt_w, tk), lambda bi, i: (bi, 0, 0, 0)),
        ],
        out_specs=pl.BlockSpec((1, tq, v_w), lambda bi, i: (bi, i, 0)),
        out_shape=jax.ShapeDtypeStruct((b, s, v_w), BF16),
        scratch_shapes=[pltpu.VMEM((MLA_HEADS, 1, tq), F32), pltpu.VMEM((MLA_HEADS, MLA_VT_ROWS, tq), F32)],
        compiler_params=_cparams("parallel", "arbitrary"),
        name="mla_attention",
    )(q, k, vt)


def _block_diag(x, nblk, bw):
    blk = lax.broadcasted_iota(jnp.int32, x.shape, 1) // bw
    return jnp.concatenate([jnp.where(blk == h, x, 0.0) for h in range(nblk)], axis=0)


def _dn_kernel(x_ref, halo_ref, ab_ref, cw_ref, prm_ref, og_ref, y_ref,
               s_ref, xs_ref, q_s, k_s, v_s, g_s, b_s, cg_s, bm_s, hm_s, egl_s, *, tc):
    i = pl.program_id(1)
    nh, hd, ck = DN_HEADS, DN_HD, DN_CHUNK

    @pl.when(i == 0)
    def _():
        s_ref[...] = jnp.zeros(s_ref.shape, F32)

    xs_ref[0:DN_HALO, :] = jnp.where(i > 0, halo_ref[0].astype(F32), 0.0)
    xs_ref[DN_HALO:DN_HALO + tc, :] = x_ref[0].astype(F32)
    neg_a = -jnp.exp(prm_ref[0:1, :])
    lane = lax.broadcasted_iota(jnp.int32, (ck, LANE), 1)

    def mix_chunk(c):
        r0 = c * ck
        for cb in range(3 * nh):
            cols = slice(hd * cb, hd * (cb + 1))
            y = None
            for j in range(DN_CONV):
                t0 = DN_HALO + r0 - (DN_CONV - 1) + j
                tap = xs_ref[t0:t0 + ck, cols] * cw_ref[j:j + 1, cols]
                y = tap if y is None else y + tap
            y = y * jax.nn.sigmoid(y)
            if cb < 2 * nh:
                y = y * lax.rsqrt(jnp.sum(y * y, axis=-1, keepdims=True) + 1e-6)
            if cb < nh:
                q_s[r0:r0 + ck, cols] = y * (hd ** -0.5)
            elif cb < 2 * nh:
                k_s[r0:r0 + ck, hd * (cb - nh):hd * (cb - nh + 1)] = y
            else:
                v_s[r0:r0 + ck, hd * (cb - 2 * nh):hd * (cb - 2 * nh + 1)] = y
            yield
        ab = ab_ref[0, r0:r0 + ck, :]
        xg = ab + prm_ref[1:2, :]
        softplus = jnp.maximum(xg, 0.0) + jnp.log1p(jnp.exp(-jnp.abs(xg)))
        g_s[r0:r0 + ck, :] = jnp.where(lane < nh, neg_a * softplus, 0.0)
        b_s[r0:r0 + ck, :] = jax.nn.sigmoid(ab)

    cat = nh * ck
    ri = lax.broadcasted_iota(jnp.int32, (ck, ck), 0)
    ci = lax.broadcasted_iota(jnp.int32, (ck, ck), 1)
    tri = jnp.where(ri >= ci, 1.0, 0.0).astype(BF16)
    ones = jnp.ones((ck, ck), BF16)
    ii = lax.broadcasted_iota(jnp.int32, (ck, cat), 0)
    lane_c = lax.broadcasted_iota(jnp.int32, (ck, cat), 1)
    jj = lane_c % ck
    blk_c = lane_c // ck
    blk_k = lax.broadcasted_iota(jnp.int32, (ck, nh * hd), 1) // hd
    og = og_ref[...]

    def prep_chunk(c):
        rows = slice(c * ck, (c + 1) * ck)
        qc, kc, vc = q_s[rows, :], k_s[rows, :], v_s[rows, :]
        gc = _dot_split3(tri, g_s[rows, :])
        yield
        bch = b_s[rows, :]
        glast = gc[ck - 1:ck, :]

        def cols_to_heads(a, lane0, width):
            return jnp.concatenate(
                [jnp.broadcast_to(a[:, lane0 + h:lane0 + h + 1], (a.shape[0], width)) for h in range(nh)], axis=1)

        gc_w = cols_to_heads(gc, 0, hd)
        beta_w = cols_to_heads(bch, nh, hd)
        glast_w = cols_to_heads(glast, 0, hd)
        eg = jnp.exp(gc_w)
        kb = kc * beta_w
        vb = vc * beta_w
        qg = qc * eg
        kbg = kb * eg
        kd = kc * jnp.exp(glast_w - gc_w)

        colcat = jnp.zeros((ck, cat), F32)
        for h in range(nh):
            colcat = jnp.where(blk_c == h, jnp.broadcast_to(gc[:, h:h + 1], (ck, cat)), colcat)
        rowcat = _dot_split3(ones, jnp.where(ii == jj, colcat, 0.0))
        yield
        diff = colcat - rowcat
        d_inc = jnp.exp(jnp.where(ii >= jj, diff, -jnp.inf))
        d_str = jnp.where(ii > jj, d_inc, 0.0)

        lhs = jnp.concatenate([kb, qc], axis=0).astype(BF16)
        bdk = jnp.concatenate([jnp.where(blk_k == h, kc, 0.0) for h in range(nh)], axis=0).astype(BF16)
        aq = _nt(lhs, bdk)
        yield
        low = aq[:ck] * d_str
        qk = aq[ck:] * d_inc

        m = -low
        p = low
        bdp = _block_diag(p, nh, ck).astype(BF16)
        n_sq = ck.bit_length() - 2
        for _ in range(n_sq):
            p = _dot(p.astype(BF16), bdp)
            yield
            bdp = _block_diag(p, nh, ck).astype(BF16)
            m = m + p + _dot(m.astype(BF16), bdp)
        yield

        rhs = jnp.concatenate(
            [jnp.concatenate([vb[:, hd * h:hd * (h + 1)], kbg[:, hd * h:hd * (h + 1)]], axis=1) for h in range(nh)],
            axis=0)
        sol = rhs + _dot(_block_diag(m, nh, ck).astype(BF16), rhs.astype(BF16))
        yield
        sol_b = sol.astype(BF16)
        qkuw = _dot(_block_diag(qk, nh, ck).astype(BF16), sol_b)
        cg, bmat = [], []
        for h in range(nh):
            hs = slice(ck * h, ck * (h + 1))
            bc = lax.dot_general(kd[:, hd * h:hd * (h + 1)].astype(BF16), sol_b[hs], (((0,), (0,)), ((), ())),
                                 preferred_element_type=F32)
            bmat.append(bc[:, :hd])
            gmat = qg[:, hd * h:hd * (h + 1)] - qkuw[hs, hd:]
            cg.append(jnp.concatenate([bc[:, hd:], gmat], axis=0).astype(BF16))
        return cg, bmat, qkuw[:, :hd], jnp.broadcast_to(jnp.exp(glast), (8, LANE))

    def recur_chunks(chunks):
        for c in chunks:
            rows = slice(c * ck, (c + 1) * ck)
            egl = egl_s[c]
            rs = [_dot(cg_s[c, h], s_ref[h].astype(BF16)) for h in range(nh)]
            yield
            for h in range(nh):
                s_ref[h] = s_ref[h] * egl[0:1, h:h + 1] - rs[h][:hd] + bm_s[c, h]
                o = rs[h][hd:] + hm_s[c, ck * h:ck * (h + 1), :]
                o = o * lax.rsqrt(jnp.mean(o * o, axis=-1, keepdims=True) + RMS_EPS) * og
                y_ref[0, rows, hd * h:hd * (h + 1)] = o.astype(y_ref.dtype)
            yield

    unroll = DN_PREP_UNROLL
    groups = [list(range(g0, g0 + unroll)) for g0 in range(0, tc // ck, unroll)]
    for step in range(len(groups) + 2):
        gens = []
        if step - 1 in range(len(groups)):
            prep_ids = groups[step - 1]
            gens += [prep_chunk(c) for c in prep_ids]
        if step in range(len(groups)):
            gens += [mix_chunk(c) for c in groups[step]]
        if step - 2 in range(len(groups)):
            gens.append(recur_chunks(groups[step - 2]))
        results = _run_staged(gens, skew=0)
        if step - 1 in range(len(groups)):
            for c, (cgm, bmat, hmat, egl) in zip(prep_ids, results):
                for h in range(nh):
                    cg_s[c, h] = cgm[h]
                    bm_s[c, h] = bmat[h]
                hm_s[c] = hmat
                egl_s[c] = egl


def _deltanet(dnqkv, dnab, conv_w, a_log, dt_bias, out_norm_g):
    b, s, _ = dnqkv.shape
    tc = min(s, DN_TILE)
    hb = tc // DN_HALO
    nck = tc // DN_CHUNK
    assert nck % DN_PREP_UNROLL == 0
    prm = jnp.zeros((8, LANE), F32).at[0, :DN_HEADS].set(a_log).at[1, :DN_HEADS].set(dt_bias)
    return pl.pallas_call(
        functools.partial(_dn_kernel, tc=tc),
        grid=(b, s // tc),
        in_specs=[
            pl.BlockSpec((1, tc, W_DNQKV), lambda bi, i: (bi, i, 0)),
            pl.BlockSpec((1, DN_HALO, W_DNQKV), lambda bi, i: (bi, jnp.maximum(i * hb - 1, 0), 0)),
            pl.BlockSpec((1, tc, LANE), lambda bi, i: (bi, i, 0)),
            pl.BlockSpec((DN_CONV, W_DNQKV), lambda bi, i: (0, 0)),
            pl.BlockSpec((8, LANE), lambda bi, i: (0, 0)),
            pl.BlockSpec((1, DN_HD), lambda bi, i: (0, 0)),
        ],
        out_specs=pl.BlockSpec((1, tc, DN_WIDTH), lambda bi, i: (bi, i, 0)),
        out_shape=jax.ShapeDtypeStruct((b, s, DN_WIDTH), BF16),
        scratch_shapes=[
            pltpu.VMEM((DN_HEADS, DN_HD, DN_HD), F32),
            pltpu.VMEM((DN_HALO + tc, W_DNQKV), F32),
            pltpu.VMEM((tc, DN_WIDTH), F32), pltpu.VMEM((tc, DN_WIDTH), F32), pltpu.VMEM((tc, DN_WIDTH), F32),
            pltpu.VMEM((tc, LANE), F32), pltpu.VMEM((tc, LANE), F32),
            pltpu.VMEM((nck, DN_HEADS, DN_HD + DN_CHUNK, DN_HD), BF16),
            pltpu.VMEM((nck, DN_HEADS, DN_HD, DN_HD), F32),
            pltpu.VMEM((nck, DN_HEADS * DN_CHUNK, DN_HD), F32),
            pltpu.VMEM((nck, 8, LANE), F32),
        ],
        compiler_params=_cparams("parallel", "arbitrary"),
        name="gated_deltanet",
    )(dnqkv, dnqkv, dnab, conv_w, prm, out_norm_g.reshape(1, DN_HD))


_DIL_PREP_ROWS = 1024


def _dil_kernel(q_ref, k_ref, v_ref, cos_ref, sin_ref, gq_ref, gk_ref, o_ref,
                qf, kf, vf, m_run, l_run, acc, tmp, *, seq):
    g = pl.program_id(2)
    blk = DIL_BLOCK
    rt = min(seq, _DIL_PREP_ROWS)
    ones_sq = jnp.ones((DIL_HD, DIL_HD), BF16)

    def prep_tile(t):
        def norm_rope(x, gain, cs, sn):
            ms = _dot((x * x).astype(BF16), ones_sq) * (1.0 / DIL_HD)
            x = x * lax.rsqrt(ms + RMS_EPS) * gain
            return x * cs + pltpu.roll(x, DIL_HD // 2, 1) * sn

        out = []
        for part in range(rt // blk):
            rows = pl.ds(pl.multiple_of(t * rt + part * blk, blk), blk)
            cs, sn = cos_ref[0, rows, :], sin_ref[0, rows, :]
            q_new = norm_rope(q_ref[0, rows, :].astype(F32), gq_ref[...], cs, sn) * (DIL_HD ** -0.5 * LOG2_E)
            k_new = norm_rope(k_ref[0, rows, :].astype(F32), gk_ref[...], cs, sn)
            out.append((rows, q_new, k_new, v_ref[0, rows, :].astype(F32)))
            yield
        return out

    def prep_store(result):
        for rows, q_new, k_new, v_new in result:
            qf[rows, :] = q_new
            kf[rows, :] = k_new
            vf[rows, :] = v_new

    def prep_only(t, carry):
        prep_store(_run_staged([prep_tile(t)], skew=0)[0])
        return carry

    qi = lax.broadcasted_iota(jnp.int32, (blk, 2 * blk), 0)
    kj = lax.broadcasted_iota(jnp.int32, (blk, 2 * blk), 1)
    band = jnp.logical_and(kj >= qi, kj <= qi + blk)
    prev_half = kj < blk

    fct = DIL_RELAYOUT_STRIDE
    quarter = seq // fct
    cls_len = seq // (fct * fct)
    rl = min(cls_len, 256)

    def class_major(src):
        for r1 in range(fct):
            for c0 in range(0, quarter, rl):
                tmp[r1 * quarter + c0:r1 * quarter + c0 + rl, :] = src[pl.ds(r1 + c0 * fct, rl, stride=fct), :]
        for r1 in range(fct):
            for r2 in range(fct):
                for c0 in range(0, cls_len, rl):
                    d0 = (r1 * fct + r2) * cls_len + c0
                    src[d0:d0 + rl, :] = tmp[pl.ds(r1 * quarter + r2 + c0 * fct, rl, stride=fct), :]

    def token_major(src):
        for r1 in range(fct):
            for r2 in range(fct):
                for c0 in range(0, cls_len, rl):
                    d0 = (r1 * fct + r2) * cls_len + c0
                    tmp[pl.ds(r1 * quarter + r2 + c0 * fct, rl, stride=fct), :] = src[d0:d0 + rl, :]
        for r1 in range(fct):
            for c0 in range(0, quarter, rl):
                src[pl.ds(r1 + c0 * fct, rl, stride=fct), :] = tmp[r1 * quarter + c0:r1 * quarter + c0 + rl, :]

    def group(dil, first, relayout):
        nb = seq // (dil * blk)
        if relayout:
            assert first and dil == fct * fct
            lax.fori_loop(0, seq // rt, prep_only, 0)
            for ref in (qf, kf, vf):
                class_major(ref)

        def rows_at(start):
            return pl.ds(start, blk) if (dil == 1 or relayout) else pl.ds(start, blk, stride=dil)

        def block(idx):
            r = idx // nb
            n = idx % nb
            if relayout:
                start = pl.multiple_of(idx * blk, blk)
                pstart = pl.multiple_of(jnp.maximum(idx - 1, 0) * blk, blk)
            else:
                start = n * (blk * dil) + r
                pstart = jnp.maximum(start - blk * dil, r)
            rows = rows_at(start)
            prow = rows_at(pstart)
            qb = qf[rows, :].astype(BF16)
            kcat = jnp.concatenate([kf[prow, :], kf[rows, :]], axis=0).astype(BF16)
            vcat = jnp.concatenate([vf[prow, :], vf[rows, :]], axis=0).astype(BF16)
            s = _nt(qb, kcat)
            yield
            no_prev = jnp.where(n > 0, 0.0, -jnp.inf)
            s = jnp.where(band, s + jnp.where(prev_half, no_prev, 0.0), -jnp.inf)
            m_b = jnp.max(s, axis=-1, keepdims=True)
            yield
            p = jnp.exp2(s - m_b)
            l_b = jnp.sum(p, axis=-1, keepdims=True)
            pv = _dot(p.astype(BF16), vcat)
            yield
            if first:
                return rows, jnp.broadcast_to(m_b, (blk, DIL_HD)), jnp.broadcast_to(l_b, (blk, DIL_HD)), pv
            m_old = m_run[rows, :]
            m_new = jnp.maximum(m_old, m_b)
            a_old = jnp.exp2(m_old - m_new)
            a_b = jnp.exp2(m_b - m_new)
            return rows, m_new, l_run[rows, :] * a_old + l_b * a_b, acc[rows, :] * a_old + pv * a_b

        def store_blocks(results):
            for rows, m_new, l_new, acc_new in results:
                m_run[rows, :] = m_new
                l_run[rows, :] = l_new
                acc[rows, :] = acc_new

        if relayout:
            def block_group(ig, carry):
                store_blocks(_run_staged([block(ig * DIL_UNROLL + un) for un in range(DIL_UNROLL)], skew=0))
                return carry

            lax.fori_loop(0, seq // (blk * DIL_UNROLL), block_group, 0)
            for ref in (m_run, l_run, acc):
                token_major(ref)
            return

        bpt = rt // blk
        assert bpt % dil == 0
        ntiles = seq // rt

        def tile_blocks(t):
            return [block((un % dil) * nb + t * (bpt // dil) + un // dil) for un in range(bpt)]

        def tile_step(t, carry):
            results = _run_staged(tile_blocks(t) + [prep_tile(t + 1)], skew=0)
            store_blocks(results[:bpt])
            prep_store(results[bpt])
            return carry

        prep_only(0, 0)
        lax.fori_loop(0, ntiles - 1, tile_step, 0)
        store_blocks(_run_staged(tile_blocks(ntiles - 1), skew=0))

    for j, gi in enumerate(DIL_ORDER):
        dil = DIL_DILATIONS[gi]
        pl.when(g == j)(functools.partial(group, dil, j == 0, dil == DIL_RELAYOUT_STRIDE ** 2))

    @pl.when(g == DIL_GROUPS - 1)
    def _():
        def fin(t, carry):
            rows = pl.ds(pl.multiple_of(t * rt, rt), rt)
            o_ref[0, rows, :] = (acc[rows, :] / l_run[rows, :]).astype(o_ref.dtype)
            return carry

        lax.fori_loop(0, seq // rt, fin, 0)


def _dilated_attention(dil, cos_h, sin_h, gq, gk):
    b, s, _ = dil.shape
    nheads = DIL_GROUPS * DIL_HPG
    assert all(w // d == DIL_BLOCK for w, d in zip(DIL_WINDOWS, DIL_DILATIONS))
    assert s % (max(DIL_DILATIONS) * DIL_BLOCK) == 0
    assert DIL_ORDER == tuple((j + DIL_ORDER[0]) % DIL_GROUPS for j in range(DIL_GROUPS))
    grp = lambda j: (j + DIL_ORDER[0]) % DIL_GROUPS
    part = lambda p: pl.BlockSpec((1, s, DIL_HD), lambda bi, h, j: (bi, 0, p * nheads + grp(j) * DIL_HPG + h))
    tab = pl.BlockSpec((1, s, DIL_HD), lambda bi, h, g: (bi, 0, 0))
    gain = pl.BlockSpec((1, DIL_HD), lambda bi, h, g: (0, 0))
    scr = pltpu.VMEM((s, DIL_HD), F32)
    return pl.pallas_call(
        functools.partial(_dil_kernel, seq=s),
        grid=(b, DIL_HPG, DIL_GROUPS),
        in_specs=[part(0), part(1), part(2), tab, tab, gain, gain],
        out_specs=pl.BlockSpec((1, s, DIL_HD), lambda bi, h, g: (bi, 0, h)),
        out_shape=jax.ShapeDtypeStruct((b, s, DIL_HPG * DIL_HD), BF16),
        scratch_shapes=[scr] * 7,
        compiler_params=_cparams("parallel", "parallel", "arbitrary"),
        name="dilated_attention",
    )(dil, dil, dil, cos_h, sin_h, gq.reshape(1, DIL_HD), gk.reshape(1, DIL_HD))


def _merge_kernel(x_ref, ya_ref, yb_ref, yc_ref, z_ref, gate_ref, wb_ref, wo_ref, o_ref):
    d = x_ref.shape[-1]
    mixed = None
    for n, y_ref in enumerate((ya_ref, yb_ref, yc_ref)):
        z = z_ref[:, BRANCH_WIDTH * n:BRANCH_WIDTH * (n + 1)].astype(F32)
        ys = (y_ref[...].astype(F32) * (z * jax.nn.sigmoid(z))).astype(BF16)
        branch = _dot(ys, wb_ref[n])
        gate = jax.nn.sigmoid(gate_ref[:, d * n:d * (n + 1)].astype(F32))
        mixed = gate * branch if mixed is None else mixed + gate * branch
    o_ref[...] = x_ref[...] + _dot(mixed.astype(BF16), wo_ref[...])


def _merge(x2, ya, yb, yc, z, gates, w_branch, w_out):
    t, d = x2.shape
    tm = min(t, 512)
    row = lambda w: pl.BlockSpec((tm, w), lambda i: (i, 0))
    return pl.pallas_call(
        _merge_kernel,
        grid=(t // tm,),
        in_specs=[row(d), row(BRANCH_WIDTH), row(BRANCH_WIDTH), row(BRANCH_WIDTH), row(W_Z), row(N_BRANCHES * d),
                  pl.BlockSpec(w_branch.shape, lambda i: (0, 0, 0)), pl.BlockSpec(w_out.shape, lambda i: (0, 0))],
        out_specs=row(d),
        out_shape=jax.ShapeDtypeStruct((t, d), F32),
        compiler_params=_cparams("parallel"),
        name="merge_out",
    )(x2, ya, yb, yc, z, gates, w_branch, w_out)


def _prep_w_in(w_in):
    d = w_in.shape[-2]
    lead = w_in.shape[:-1]
    o_q = 0
    o_kv = o_q + MLA_Q_RANK
    o_za = o_kv + MLA_KV_RANK + MLA_ROPE
    o_dn = o_za + BRANCH_WIDTH
    o_a = o_dn + 3 * DN_WIDTH
    o_zb = o_a + 2 * DN_HEADS
    o_dil = o_zb + BRANCH_WIDTH
    o_zc = o_dil + 3 * DIL_QKV_WIDTH
    o_g = o_zc + BRANCH_WIDTH
    end = o_g + N_BRANCHES * d
    assert end == w_in.shape[-1]
    zeros = lambda n: jnp.zeros(lead + (n,), w_in.dtype)
    segs = [
        w_in[..., o_q:o_kv],
        w_in[..., o_kv:o_kv + MLA_KV_RANK], _spread_rope(w_in[..., o_kv + MLA_KV_RANK:o_za]),
        w_in[..., o_dn:o_a],
        w_in[..., o_a:o_zb], zeros(W_DNAB - 2 * DN_HEADS),
        w_in[..., o_dil:o_zc],
        w_in[..., o_za:o_dn], w_in[..., o_zb:o_dil], w_in[..., o_zc:o_g],
        w_in[..., o_g:end],
    ]
    return jnp.concatenate(segs, axis=-1).astype(BF16)


def _prep_w_q_b(w):
    lead = w.shape[:-1]
    w = w.reshape(lead + (MLA_HEADS, MLA_QK))
    w = jnp.concatenate([w[..., :MLA_NOPE], _spread_rope(w[..., MLA_NOPE:])], axis=-1)
    return w.reshape(lead + (MLA_HEADS * MLA_HEAD_PAD,)).astype(BF16)


def _prep_w_kv_b(w):
    lead = w.shape[:-1]
    w = w.reshape(lead + (MLA_HEADS, MLA_NOPE + MLA_V))
    k = w[..., :MLA_NOPE].reshape(lead + (MLA_HEADS * MLA_NOPE,))
    v = w[..., MLA_NOPE:].reshape(lead + (MLA_HEADS * MLA_V,))
    return k.astype(BF16), jnp.swapaxes(v, -1, -2).astype(BF16)


def _prep_qk_gain(g):
    return jnp.stack([g[..., :MLA_NOPE], _spread_rope(g[..., MLA_NOPE:])], axis=-2)


def kernel(x, positions, norm_g, w_in, mla_q_a_norm_g, mla_w_q_b, mla_kv_a_norm_g, mla_w_kv_b, mla_q_norm_g,
           mla_k_norm_g, dn_conv_w, dn_a_log, dn_dt_bias, dn_out_norm_g, dil_q_norm_g, dil_k_norm_g, w_branch,
           w_out):
    b, s, d = x.shape
    t = b * s
    depth = w_in.shape[0]

    cos_h, sin_h, cr, sr = _rope_tables(positions)
    cos_h3 = cos_h.reshape(b, s, LANE)
    sin_h3 = sin_h.reshape(b, s, LANE)

    w_in_p = _prep_w_in(w_in)
    w_q_p = _prep_w_q_b(mla_w_q_b)
    w_k_p, w_vt_p = _prep_w_kv_b(mla_w_kv_b)
    tk = min(s, MLA_KEY_BLOCK)
    tq = min(s, MLA_QUERY_TILE)
    gq_p = _prep_qk_gain(mla_q_norm_g)
    gk_p = _prep_qk_gain(mla_k_norm_g)
    w_branch_b = w_branch.astype(BF16)
    w_out_b = w_out.astype(BF16)

    x2 = x.reshape(t, d)
    for l in range(depth):
        qlat, kvlat, dnqkv, dnab, dil, z, gates = _in_projection(x2, norm_g[l], w_in_p[l])
        q, k, vt = _mla_prep(qlat, kvlat, cr, sr, mla_q_a_norm_g[l].reshape(1, -1), w_q_p[l],
                             mla_kv_a_norm_g[l].reshape(1, -1), w_k_p[l], w_vt_p[l], gq_p[l], gk_p[l], tk)
        y_a = _mla_attention(q.reshape(b, s, -1), k.reshape(b, s, -1),
                             vt.reshape(b, s // tk, MLA_HEADS * MLA_VT_ROWS, tk), tq, tk)
        y_b = _deltanet(dnqkv.reshape(b, s, -1), dnab.reshape(b, s, -1), dn_conv_w[l], dn_a_log[l], dn_dt_bias[l],
                        dn_out_norm_g[l])
        y_c = _dilated_attention(dil.reshape(b, s, -1), cos_h3, sin_h3, dil_q_norm_g[l], dil_k_norm_g[l])
        x2 = _merge(x2, y_a.reshape(t, -1), y_b.reshape(t, -1), y_c.reshape(t, -1), z, gates, w_branch_b[l],
                    w_out_b[l])
    return x2.reshape(b, s, d)
```

```python
import functools

import jax
import jax.numpy as jnp
from jax import lax
from jax.experimental import pallas as pl
from jax.experimental.pallas import tpu as pltpu

F32 = jnp.float32
BF16 = jnp.bfloat16

RMS_EPS = 1e-6
ROPE_THETA = 10000.0
LANE = 128

MLA_HEADS = 4
MLA_NOPE = 128
MLA_ROPE = 64
MLA_V = 128
MLA_QK = MLA_NOPE + MLA_ROPE
MLA_Q_RANK = 384
MLA_KV_RANK = 256
MLA_KV_PAD = 384
MLA_HEAD_PAD = 256
MLA_VT_ROWS = MLA_V + 16
LOG2_E = 1.4426950408889634
MLA_KEY_BLOCK = 512

DN_HEADS = 4
DN_HD = 128
DN_WIDTH = DN_HEADS * DN_HD
DN_CONV = 4
DN_CHUNK = 64
DN_HALO = 16
DN_TILE = 1024
DN_PREP_UNROLL = 4

DIL_WINDOWS = (128, 512, 2048)
DIL_DILATIONS = (1, 4, 16)
DIL_GROUPS = 3
DIL_HPG = 4
DIL_HD = 128
DIL_BLOCK = 128
DIL_QKV_WIDTH = DIL_GROUPS * DIL_HPG * DIL_HD
DIL_UNROLL = 8
DIL_ORDER = (2, 0, 1)
DIL_RELAYOUT_STRIDE = 4

N_BRANCHES = 3
BRANCH_WIDTH = 512

W_QLAT = MLA_Q_RANK
W_KVLAT = MLA_KV_PAD
W_DNQKV = 3 * DN_WIDTH
W_DNAB = LANE
W_DIL = 3 * DIL_QKV_WIDTH
W_Z = N_BRANCHES * BRANCH_WIDTH

VMEM_LIMIT = 56 * 1024 * 1024


def _cparams(*sem):
    return pltpu.CompilerParams(dimension_semantics=sem, vmem_limit_bytes=VMEM_LIMIT)


def _nt(a, b):
    return lax.dot_general(a, b, (((1,), (1,)), ((), ())), preferred_element_type=F32)


def _dot(a, b):
    return jnp.dot(a, b, preferred_element_type=F32)


def _run_staged(gens, skew):
    results = [None] * len(gens)
    done = [False] * len(gens)
    t = 0
    while not all(done):
        for n, gen in enumerate(gens):
            if n * skew <= t and not done[n]:
                try:
                    next(gen)
                except StopIteration as stop:
                    results[n] = stop.value
                    done[n] = True
        t += 1
    return results


def _dot_split3(a, x):
    hi = x.astype(BF16)
    r1 = x - hi.astype(F32)
    mid = r1.astype(BF16)
    lo = (r1 - mid.astype(F32)).astype(BF16)
    return _dot(a, hi) + _dot(a, mid) + _dot(a, lo)


def _rope_kernel(pos_ref, f_ref, cosh_ref, sinh_ref, cr_ref, sr_ref):
    half_h, half_r = DIL_HD // 2, MLA_ROPE // 2
    ang = pos_ref[...].astype(F32) * f_ref[0:1, :]
    c, s = jnp.cos(ang), jnp.sin(ang)
    c_sw, s_sw = pltpu.roll(c, half_h, 1), pltpu.roll(s, half_h, 1)
    lane = lax.broadcasted_iota(jnp.int32, ang.shape, 1)
    low = lane < half_h
    cosh_ref[...] = jnp.where(low, c, c_sw)
    sinh_ref[...] = jnp.where(low, -s, s_sw)
    first = lane < half_r
    second = jnp.logical_and(lane >= half_h, lane < half_h + half_r)
    cr_ref[...] = jnp.where(first, c_sw, jnp.where(second, c, 0.0))
    sr_ref[...] = jnp.where(first, -s_sw, jnp.where(second, s, 0.0))


def _spread_rope(a):
    half = MLA_ROPE // 2
    z = jnp.zeros(a.shape[:-1] + (LANE // 2 - half,), a.dtype)
    return jnp.concatenate([a[..., :half], z, a[..., half:], z], axis=-1)


def _rope_tables(positions):
    t = positions.size
    ts = min(t, 1024)
    inv_h = 1.0 / (ROPE_THETA ** (jnp.arange(0, DIL_HD, 2, dtype=F32) / DIL_HD))
    inv_r = 1.0 / (ROPE_THETA ** (jnp.arange(0, MLA_ROPE, 2, dtype=F32) / MLA_ROPE))
    freqs = jnp.concatenate([inv_h, inv_r, jnp.zeros(LANE - inv_h.size - inv_r.size, F32)])
    tab = jax.ShapeDtypeStruct((t, LANE), F32)
    spec = pl.BlockSpec((ts, LANE), lambda i: (i, 0))
    return pl.pallas_call(
        _rope_kernel,
        grid=(t // ts,),
        in_specs=[pl.BlockSpec((ts, 1), lambda i: (i, 0)), pl.BlockSpec((8, LANE), lambda i: (0, 0))],
        out_specs=[spec] * 4,
        out_shape=[tab] * 4,
        compiler_params=_cparams("parallel"),
        name="rope_tables",
    )(positions.reshape(t, 1), jnp.broadcast_to(freqs, (8, LANE)))


_INPROJ_COLS = 512


def _inproj_kernel(x_ref, g_ref, w_ref, *out_refs):
    x = x_ref[...]
    h = (x * lax.rsqrt(jnp.mean(x * x, axis=-1, keepdims=True) + RMS_EPS) * g_ref[...]).astype(BF16)
    off = 0
    for o_ref in out_refs:
        width = o_ref.shape[-1]
        for c0 in range(0, width, _INPROJ_COLS):
            cw = min(_INPROJ_COLS, width - c0)
            acc = _dot(h, w_ref[:, off + c0:off + c0 + cw])
            o_ref[:, c0:c0 + cw] = acc.astype(o_ref.dtype)
        off += width


def _in_projection(x2, norm_g, w_in_p):
    t, d = x2.shape
    tm = min(t, 256)
    widths = (W_QLAT, W_KVLAT, W_DNQKV, W_DNAB, W_DIL, W_Z, N_BRANCHES * d)
    dtypes = (BF16, BF16, BF16, F32, BF16, BF16, BF16)
    assert sum(widths) == w_in_p.shape[1]
    return pl.pallas_call(
        _inproj_kernel,
        grid=(t // tm,),
        in_specs=[
            pl.BlockSpec((tm, d), lambda i: (i, 0)),
            pl.BlockSpec((1, d), lambda i: (0, 0)),
            pl.BlockSpec(memory_space=pltpu.VMEM),
        ],
        out_specs=[pl.BlockSpec((tm, w), lambda i: (i, 0)) for w in widths],
        out_shape=[jax.ShapeDtypeStruct((t, w), dt) for w, dt in zip(widths, dtypes)],
        compiler_params=_cparams("parallel"),
        name="in_projection",
    )(x2, norm_g.reshape(1, d), w_in_p)


def _mla_prep_kernel(qlat_ref, kvlat_ref, cr_ref, sr_ref, gqa_ref, wq_ref, gkva_ref, wk_ref, wvt_ref,
                     gq_ref, gk_ref, q_ref, k_ref, vt_ref):
    def rms(x, gain, n):
        return x * lax.rsqrt(jnp.sum(x * x, axis=-1, keepdims=True) * (1.0 / n) + RMS_EPS) * gain

    cr, sr = cr_ref[...], sr_ref[...]

    def rope(x):
        return x * cr + pltpu.roll(x, LANE // 2, 1) * sr

    scale = MLA_QK ** -0.5 * LOG2_E
    qn = rms(qlat_ref[...].astype(F32), gqa_ref[...], MLA_Q_RANK).astype(BF16)
    q = _dot(qn, wq_ref[...])
    kvl = kvlat_ref[...].astype(F32)
    cn = rms(kvl[:, :MLA_KV_RANK], gkva_ref[...], MLA_KV_RANK).astype(BF16)
    kn = _dot(cn, wk_ref[...])
    vt = _nt(wvt_ref[...], cn).astype(BF16)
    for h in range(MLA_HEADS):
        r0 = MLA_VT_ROWS * h
        vt_ref[0, r0:r0 + MLA_V, :] = vt[MLA_V * h:MLA_V * (h + 1)]
        vt_ref[0, r0 + MLA_V:r0 + MLA_VT_ROWS, :] = jnp.ones((MLA_VT_ROWS - MLA_V, vt.shape[1]), BF16)
    k_pe = rope(rms(kvl[:, MLA_KV_RANK:], gk_ref[1:2, :], MLA_ROPE)).astype(BF16)
    for h in range(MLA_HEADS):
        c0 = MLA_HEAD_PAD * h
        q_nope = rms(q[:, c0:c0 + MLA_NOPE], gq_ref[0:1, :], MLA_NOPE)
        q_pe = rope(rms(q[:, c0 + MLA_NOPE:c0 + MLA_HEAD_PAD], gq_ref[1:2, :], MLA_ROPE))
        q_ref[:, c0:c0 + MLA_NOPE] = (q_nope * scale).astype(BF16)
        q_ref[:, c0 + MLA_NOPE:c0 + MLA_HEAD_PAD] = (q_pe * scale).astype(BF16)
        k_nope = rms(kn[:, MLA_NOPE * h:MLA_NOPE * (h + 1)], gk_ref[0:1, :], MLA_NOPE)
        k_ref[:, c0:c0 + MLA_NOPE] = k_nope.astype(BF16)
        k_ref[:, c0 + MLA_NOPE:c0 + MLA_HEAD_PAD] = k_pe


def _mla_prep(qlat, kvlat, cr, sr, gqa, wq, gkva, wk, wvt, gq, gk, tm):
    t = qlat.shape[0]
    row = lambda w: pl.BlockSpec((tm, w), lambda i: (i, 0))
    full = lambda a: pl.BlockSpec(a.shape, lambda i: (0,) * a.ndim)
    qk_w = MLA_HEADS * MLA_HEAD_PAD
    v_w = MLA_HEADS * MLA_VT_ROWS
    return pl.pallas_call(
        _mla_prep_kernel,
        grid=(t // tm,),
        in_specs=[row(W_QLAT), row(W_KVLAT), row(LANE), row(LANE),
                  full(gqa), full(wq), full(gkva), full(wk), full(wvt), full(gq), full(gk)],
        out_specs=[row(qk_w), row(qk_w), pl.BlockSpec((1, v_w, tm), lambda i: (i, 0, 0))],
        out_shape=[jax.ShapeDtypeStruct((t, qk_w), BF16), jax.ShapeDtypeStruct((t, qk_w), BF16),
                   jax.ShapeDtypeStruct((t // tm, v_w, tm), BF16)],
        compiler_params=_cparams("parallel"),
        name="mla_prep",
    )(qlat, kvlat, cr, sr, gqa, wq, gkva, wk, wvt, gq, gk)


def _mla_attn_kernel(q_ref, k_ref, vt_ref, o_ref, m_ref, acc_ref, *, tk):
    qi = pl.program_id(1)
    m_ref[...] = jnp.full(m_ref.shape, -jnp.inf, F32)
    acc_ref[...] = jnp.zeros(acc_ref.shape, F32)

    def head_step(j, h, diagonal):
        start = pl.multiple_of(j * tk, tk)
        cols = slice(MLA_HEAD_PAD * h, MLA_HEAD_PAD * (h + 1))
        st = _nt(k_ref[0, pl.ds(start, tk), cols], q_ref[0, :, cols])
        yield
        if diagonal:
            key = lax.broadcasted_iota(jnp.int32, st.shape, 0)
            qry = lax.broadcasted_iota(jnp.int32, st.shape, 1)
            st = jnp.where(key <= qry, st, -jnp.inf)
        m_old = m_ref[h]
        m_new = jnp.maximum(m_old, jnp.max(st, axis=0, keepdims=True))
        yield
        alpha = jnp.exp2(m_old - m_new)
        p = jnp.exp2(st - m_new)
        yield
        p = p.astype(BF16)
        acc_old = alpha * acc_ref[h]
        yield
        vt = vt_ref[0, j, MLA_VT_ROWS * h:MLA_VT_ROWS * (h + 1), :]
        acc_ref[h] = acc_old + _dot(vt, p)
        m_ref[h] = m_new

    def step(j, diagonal):
        _run_staged([head_step(j, h, diagonal) for h in range(MLA_HEADS)], skew=1)

    def body(j, carry):
        step(j, False)
        return carry

    lax.fori_loop(0, qi, body, 0)
    step(qi, True)

    for h in range(MLA_HEADS):
        o = acc_ref[h, :MLA_V, :] / acc_ref[h, MLA_V:MLA_V + 1, :]
        o_ref[0, :, MLA_V * h:MLA_V * (h + 1)] = o.T.astype(o_ref.dtype)


def _mla_attention(q, k, vt, tk):
    b, s, qk_w = q.shape
    v_w = MLA_HEADS * MLA_V
    vt_w = MLA_HEADS * MLA_VT_ROWS
    tq = tk
    return pl.pallas_call(
        functools.partial(_mla_attn_kernel, tk=tk),
        grid=(b, s // tq),
        in_specs=[
            pl.BlockSpec((1, tq, qk_w), lambda bi, i: (bi, i, 0)),
            pl.BlockSpec((1, s, qk_w), lambda bi, i: (bi, 0, 0)),
            pl.BlockSpec((1, s // tk, vt_w, tk), lambda bi, i: (bi, 0, 0, 0)),
        ],
        out_specs=pl.BlockSpec((1, tq, v_w), lambda bi, i: (bi, i, 0)),
        out_shape=jax.ShapeDtypeStruct((b, s, v_w), BF16),
        scratch_shapes=[pltpu.VMEM((MLA_HEADS, 1, tq), F32), pltpu.VMEM((MLA_HEADS, MLA_VT_ROWS, tq), F32)],
        compiler_params=_cparams("parallel", "arbitrary"),
        name="mla_attention",
    )(q, k, vt)


def _block_diag(x, nblk, bw):
    blk = lax.broadcasted_iota(jnp.int32, x.shape, 1) // bw
    return jnp.concatenate([jnp.where(blk == h, x, 0.0) for h in range(nblk)], axis=0)


def _dn_kernel(x_ref, halo_ref, ab_ref, cw_ref, prm_ref, og_ref, y_ref,
               s_ref, xs_ref, q_s, k_s, v_s, g_s, b_s, cg_s, bm_s, hm_s, egl_s, *, tc):
    i = pl.program_id(1)
    nh, hd, ck = DN_HEADS, DN_HD, DN_CHUNK

    @pl.when(i == 0)
    def _():
        s_ref[...] = jnp.zeros(s_ref.shape, F32)

    xs_ref[0:DN_HALO, :] = jnp.where(i > 0, halo_ref[0].astype(F32), 0.0)
    xs_ref[DN_HALO:DN_HALO + tc, :] = x_ref[0].astype(F32)
    neg_a = -jnp.exp(prm_ref[0:1, :])
    lane = lax.broadcasted_iota(jnp.int32, (ck, LANE), 1)

    def mix_chunk(c):
        r0 = c * ck
        for cb in range(3 * nh):
            cols = slice(hd * cb, hd * (cb + 1))
            y = None
            for j in range(DN_CONV):
                t0 = DN_HALO + r0 - (DN_CONV - 1) + j
                tap = xs_ref[t0:t0 + ck, cols] * cw_ref[j:j + 1, cols]
                y = tap if y is None else y + tap
            y = y * jax.nn.sigmoid(y)
            if cb < 2 * nh:
                y = y * lax.rsqrt(jnp.sum(y * y, axis=-1, keepdims=True) + 1e-6)
            if cb < nh:
                q_s[r0:r0 + ck, cols] = y * (hd ** -0.5)
            elif cb < 2 * nh:
                k_s[r0:r0 + ck, hd * (cb - nh):hd * (cb - nh + 1)] = y
            else:
                v_s[r0:r0 + ck, hd * (cb - 2 * nh):hd * (cb - 2 * nh + 1)] = y
            yield
        ab = ab_ref[0, r0:r0 + ck, :]
        xg = ab + prm_ref[1:2, :]
        softplus = jnp.maximum(xg, 0.0) + jnp.log1p(jnp.exp(-jnp.abs(xg)))
        g_s[r0:r0 + ck, :] = jnp.where(lane < nh, neg_a * softplus, 0.0)
        b_s[r0:r0 + ck, :] = jax.nn.sigmoid(ab)

    cat = nh * ck
    ri = lax.broadcasted_iota(jnp.int32, (ck, ck), 0)
    ci = lax.broadcasted_iota(jnp.int32, (ck, ck), 1)
    tri = jnp.where(ri >= ci, 1.0, 0.0).astype(BF16)
    ones = jnp.ones((ck, ck), BF16)
    ii = lax.broadcasted_iota(jnp.int32, (ck, cat), 0)
    lane_c = lax.broadcasted_iota(jnp.int32, (ck, cat), 1)
    jj = lane_c % ck
    blk_c = lane_c // ck
    blk_k = lax.broadcasted_iota(jnp.int32, (ck, nh * hd), 1) // hd
    og = og_ref[...]

    def prep_chunk(c):
        rows = slice(c * ck, (c + 1) * ck)
        qc, kc, vc = q_s[rows, :], k_s[rows, :], v_s[rows, :]
        gc = _dot_split3(tri, g_s[rows, :])
        yield
        bch = b_s[rows, :]
        glast = gc[ck - 1:ck, :]

        def cols_to_heads(a, lane0, width):
            return jnp.concatenate(
                [jnp.broadcast_to(a[:, lane0 + h:lane0 + h + 1], (a.shape[0], width)) for h in range(nh)], axis=1)

        gc_w = cols_to_heads(gc, 0, hd)
        beta_w = cols_to_heads(bch, nh, hd)
        glast_w = cols_to_heads(glast, 0, hd)
        eg = jnp.exp(gc_w)
        kb = kc * beta_w
        vb = vc * beta_w
        qg = qc * eg
        kbg = kb * eg
        kd = kc * jnp.exp(glast_w - gc_w)

        colcat = jnp.zeros((ck, cat), F32)
        for h in range(nh):
            colcat = jnp.where(blk_c == h, jnp.broadcast_to(gc[:, h:h + 1], (ck, cat)), colcat)
        rowcat = _dot_split3(ones, jnp.where(ii == jj, colcat, 0.0))
        yield
        diff = colcat - rowcat
        d_inc = jnp.exp(jnp.where(ii >= jj, diff, -jnp.inf))
        d_str = jnp.where(ii > jj, d_inc, 0.0)

        lhs = jnp.concatenate([kb, qc], axis=0).astype(BF16)
        bdk = jnp.concatenate([jnp.where(blk_k == h, kc, 0.0) for h in range(nh)], axis=0).astype(BF16)
        aq = _nt(lhs, bdk)
        yield
        low = aq[:ck] * d_str
        qk = aq[ck:] * d_inc

        m = -low
        p = low
        bdp = _block_diag(p, nh, ck).astype(BF16)
        n_sq = ck.bit_length() - 2
        for _ in range(n_sq):
            p = _dot(p.astype(BF16), bdp)
            yield
            bdp = _block_diag(p, nh, ck).astype(BF16)
            m = m + p + _dot(m.astype(BF16), bdp)
        yield

        rhs = jnp.concatenate(
            [jnp.concatenate([vb[:, hd * h:hd * (h + 1)], kbg[:, hd * h:hd * (h + 1)]], axis=1) for h in range(nh)],
            axis=0)
        sol = rhs + _dot(_block_diag(m, nh, ck).astype(BF16), rhs.astype(BF16))
        yield
        sol_b = sol.astype(BF16)
        qkuw = _dot(_block_diag(qk, nh, ck).astype(BF16), sol_b)
        cg, bmat = [], []
        for h in range(nh):
            hs = slice(ck * h, ck * (h + 1))
            bc = lax.dot_general(kd[:, hd * h:hd * (h + 1)].astype(BF16), sol_b[hs], (((0,), (0,)), ((), ())),
                                 preferred_element_type=F32)
            bmat.append(bc[:, :hd])
            gmat = qg[:, hd * h:hd * (h + 1)] - qkuw[hs, hd:]
            cg.append(jnp.concatenate([bc[:, hd:], gmat], axis=0).astype(BF16))
        return cg, bmat, qkuw[:, :hd], jnp.broadcast_to(jnp.exp(glast), (8, LANE))

    def recur_chunks(chunks):
        for c in chunks:
            rows = slice(c * ck, (c + 1) * ck)
            egl = egl_s[c]
            rs = [_dot(cg_s[c, h], s_ref[h].astype(BF16)) for h in range(nh)]
            yield
            for h in range(nh):
                s_ref[h] = s_ref[h] * egl[0:1, h:h + 1] - rs[h][:hd] + bm_s[c, h]
                o = rs[h][hd:] + hm_s[c, ck * h:ck * (h + 1), :]
                o = o * lax.rsqrt(jnp.mean(o * o, axis=-1, keepdims=True) + RMS_EPS) * og
                y_ref[0, rows, hd * h:hd * (h + 1)] = o.astype(y_ref.dtype)
            yield

    unroll = DN_PREP_UNROLL
    groups = [list(range(g0, g0 + unroll)) for g0 in range(0, tc // ck, unroll)]
    for step in range(len(groups) + 2):
        gens = []
        if step - 1 in range(len(groups)):
            prep_ids = groups[step - 1]
            gens += [prep_chunk(c) for c in prep_ids]
        if step in range(len(groups)):
            gens += [mix_chunk(c) for c in groups[step]]
        if step - 2 in range(len(groups)):
            gens.append(recur_chunks(groups[step - 2]))
        results = _run_staged(gens, skew=0)
        if step - 1 in range(len(groups)):
            for c, (cgm, bmat, hmat, egl) in zip(prep_ids, results):
                for h in range(nh):
                    cg_s[c, h] = cgm[h]
                    bm_s[c, h] = bmat[h]
                hm_s[c] = hmat
                egl_s[c] = egl


def _deltanet(dnqkv, dnab, conv_w, a_log, dt_bias, out_norm_g):
    b, s, _ = dnqkv.shape
    tc = min(s, DN_TILE)
    hb = tc // DN_HALO
    nck = tc // DN_CHUNK
    assert nck % DN_PREP_UNROLL == 0
    prm = jnp.zeros((8, LANE), F32).at[0, :DN_HEADS].set(a_log).at[1, :DN_HEADS].set(dt_bias)
    return pl.pallas_call(
        functools.partial(_dn_kernel, tc=tc),
        grid=(b, s // tc),
        in_specs=[
            pl.BlockSpec((1, tc, W_DNQKV), lambda bi, i: (bi, i, 0)),
            pl.BlockSpec((1, DN_HALO, W_DNQKV), lambda bi, i: (bi, jnp.maximum(i * hb - 1, 0), 0)),
            pl.BlockSpec((1, tc, LANE), lambda bi, i: (bi, i, 0)),
            pl.BlockSpec((DN_CONV, W_DNQKV), lambda bi, i: (0, 0)),
            pl.BlockSpec((8, LANE), lambda bi, i: (0, 0)),
            pl.BlockSpec((1, DN_HD), lambda bi, i: (0, 0)),
        ],
        out_specs=pl.BlockSpec((1, tc, DN_WIDTH), lambda bi, i: (bi, i, 0)),
        out_shape=jax.ShapeDtypeStruct((b, s, DN_WIDTH), BF16),
        scratch_shapes=[
            pltpu.VMEM((DN_HEADS, DN_HD, DN_HD), F32),
            pltpu.VMEM((DN_HALO + tc, W_DNQKV), F32),
            pltpu.VMEM((tc, DN_WIDTH), F32), pltpu.VMEM((tc, DN_WIDTH), F32), pltpu.VMEM((tc, DN_WIDTH), F32),
            pltpu.VMEM((tc, LANE), F32), pltpu.VMEM((tc, LANE), F32),
            pltpu.VMEM((nck, DN_HEADS, DN_HD + DN_CHUNK, DN_HD), BF16),
            pltpu.VMEM((nck, DN_HEADS, DN_HD, DN_HD), F32),
            pltpu.VMEM((nck, DN_HEADS * DN_CHUNK, DN_HD), F32),
            pltpu.VMEM((nck, 8, LANE), F32),
        ],
        compiler_params=_cparams("parallel", "arbitrary"),
        name="gated_deltanet",
    )(dnqkv, dnqkv, dnab, conv_w, prm, out_norm_g.reshape(1, DN_HD))


_DIL_PREP_ROWS = 1024


def _dil_kernel(q_ref, k_ref, v_ref, cos_ref, sin_ref, gq_ref, gk_ref, o_ref,
                qf, kf, vf, m_run, l_run, acc, tmp, *, seq):
    g = pl.program_id(2)
    blk = DIL_BLOCK
    rt = min(seq, _DIL_PREP_ROWS)
    ones_sq = jnp.ones((DIL_HD, DIL_HD), BF16)

    def prep_tile(t):
        def norm_rope(x, gain, cs, sn):
            ms = _dot((x * x).astype(BF16), ones_sq) * (1.0 / DIL_HD)
            x = x * lax.rsqrt(ms + RMS_EPS) * gain
            return x * cs + pltpu.roll(x, DIL_HD // 2, 1) * sn

        out = []
        for part in range(rt // blk):
            rows = pl.ds(pl.multiple_of(t * rt + part * blk, blk), blk)
            cs, sn = cos_ref[0, rows, :], sin_ref[0, rows, :]
            q_new = norm_rope(q_ref[0, rows, :].astype(F32), gq_ref[...], cs, sn) * (DIL_HD ** -0.5 * LOG2_E)
            k_new = norm_rope(k_ref[0, rows, :].astype(F32), gk_ref[...], cs, sn)
            out.append((rows, q_new, k_new, v_ref[0, rows, :].astype(F32)))
            yield
        return out

    def prep_store(result):
        for rows, q_new, k_new, v_new in result:
            qf[rows, :] = q_new
            kf[rows, :] = k_new
            vf[rows, :] = v_new

    def prep_only(t, carry):
        prep_store(_run_staged([prep_tile(t)], skew=0)[0])
        return carry

    qi = lax.broadcasted_iota(jnp.int32, (blk, 2 * blk), 0)
    kj = lax.broadcasted_iota(jnp.int32, (blk, 2 * blk), 1)
    band = jnp.logical_and(kj >= qi, kj <= qi + blk)
    prev_half = kj < blk

    fct = DIL_RELAYOUT_STRIDE
    quarter = seq // fct
    cls_len = seq // (fct * fct)
    rl = min(cls_len, 256)

    def class_major(src):
        for r1 in range(fct):
            for c0 in range(0, quarter, rl):
                tmp[r1 * quarter + c0:r1 * quarter + c0 + rl, :] = src[pl.ds(r1 + c0 * fct, rl, stride=fct), :]
        for r1 in range(fct):
            for r2 in range(fct):
                for c0 in range(0, cls_len, rl):
                    d0 = (r1 * fct + r2) * cls_len + c0
                    src[d0:d0 + rl, :] = tmp[pl.ds(r1 * quarter + r2 + c0 * fct, rl, stride=fct), :]

    def token_major(src):
        for r1 in range(fct):
            for r2 in range(fct):
                for c0 in range(0, cls_len, rl):
                    d0 = (r1 * fct + r2) * cls_len + c0
                    tmp[pl.ds(r1 * quarter + r2 + c0 * fct, rl, stride=fct), :] = src[d0:d0 + rl, :]
        for r1 in range(fct):
            for c0 in range(0, quarter, rl):
                src[pl.ds(r1 + c0 * fct, rl, stride=fct), :] = tmp[r1 * quarter + c0:r1 * quarter + c0 + rl, :]

    def group(dil, first, relayout):
        nb = seq // (dil * blk)
        if relayout:
            assert first and dil == fct * fct
            lax.fori_loop(0, seq // rt, prep_only, 0)
            for ref in (qf, kf, vf):
                class_major(ref)

        def rows_at(start):
            return pl.ds(start, blk) if (dil == 1 or relayout) else pl.ds(start, blk, stride=dil)

        def block(idx):
            r = idx // nb
            n = idx % nb
            if relayout:
                start = pl.multiple_of(idx * blk, blk)
                pstart = pl.multiple_of(jnp.maximum(idx - 1, 0) * blk, blk)
            else:
                start = n * (blk * dil) + r
                pstart = jnp.maximum(start - blk * dil, r)
            rows = rows_at(start)
            prow = rows_at(pstart)
            qb = qf[rows, :].astype(BF16)
            kcat = jnp.concatenate([kf[prow, :], kf[rows, :]], axis=0).astype(BF16)
            vcat = jnp.concatenate([vf[prow, :], vf[rows, :]], axis=0).astype(BF16)
            s = _nt(qb, kcat)
            yield
            no_prev = jnp.where(n > 0, 0.0, -jnp.inf)
            s = jnp.where(band, s + jnp.where(prev_half, no_prev, 0.0), -jnp.inf)
            m_b = jnp.max(s, axis=-1, keepdims=True)
            yield
            p = jnp.exp2(s - m_b)
            l_b = jnp.sum(p, axis=-1, keepdims=True)
            pv = _dot(p.astype(BF16), vcat)
            yield
            if first:
                return rows, jnp.broadcast_to(m_b, (blk, DIL_HD)), jnp.broadcast_to(l_b, (blk, DIL_HD)), pv
            m_old = m_run[rows, :]
            m_new = jnp.maximum(m_old, m_b)
            a_old = jnp.exp2(m_old - m_new)
            a_b = jnp.exp2(m_b - m_new)
            return rows, m_new, l_run[rows, :] * a_old + l_b * a_b, acc[rows, :] * a_old + pv * a_b

        def store_blocks(results):
            for rows, m_new, l_new, acc_new in results:
                m_run[rows, :] = m_new
                l_run[rows, :] = l_new
                acc[rows, :] = acc_new

        if relayout:
            def block_group(ig, carry):
                store_blocks(_run_staged([block(ig * DIL_UNROLL + un) for un in range(DIL_UNROLL)], skew=0))
                return carry

            lax.fori_loop(0, seq // (blk * DIL_UNROLL), block_group, 0)
            for ref in (m_run, l_run, acc):
                token_major(ref)
            return

        bpt = rt // blk
        assert bpt % dil == 0
        ntiles = seq // rt

        def tile_blocks(t):
            return [block((un % dil) * nb + t * (bpt // dil) + un // dil) for un in range(bpt)]

        def tile_step(t, carry):
            results = _run_staged(tile_blocks(t) + [prep_tile(t + 1)], skew=0)
            store_blocks(results[:bpt])
            prep_store(results[bpt])
            return carry

        prep_only(0, 0)
        lax.fori_loop(0, ntiles - 1, tile_step, 0)
        store_blocks(_run_staged(tile_blocks(ntiles - 1), skew=0))

    for j, gi in enumerate(DIL_ORDER):
        dil = DIL_DILATIONS[gi]
        pl.when(g == j)(functools.partial(group, dil, j == 0, dil == DIL_RELAYOUT_STRIDE ** 2))

    @pl.when(g == DIL_GROUPS - 1)
    def _():
        def fin(t, carry):
            rows = pl.ds(pl.multiple_of(t * rt, rt), rt)
            o_ref[0, rows, :] = (acc[rows, :] / l_run[rows, :]).astype(o_ref.dtype)
            return carry

        lax.fori_loop(0, seq // rt, fin, 0)


def _dilated_attention(dil, cos_h, sin_h, gq, gk):
    b, s, _ = dil.shape
    nheads = DIL_GROUPS * DIL_HPG
    assert all(w // d == DIL_BLOCK for w, d in zip(DIL_WINDOWS, DIL_DILATIONS))
    assert s % (max(DIL_DILATIONS) * DIL_BLOCK) == 0
    assert DIL_ORDER == tuple((j + DIL_ORDER[0]) % DIL_GROUPS for j in range(DIL_GROUPS))
    grp = lambda j: (j + DIL_ORDER[0]) % DIL_GROUPS
    part = lambda p: pl.BlockSpec((1, s, DIL_HD), lambda bi, h, j: (bi, 0, p * nheads + grp(j) * DIL_HPG + h))
    tab = pl.BlockSpec((1, s, DIL_HD), lambda bi, h, g: (bi, 0, 0))
    gain = pl.BlockSpec((1, DIL_HD), lambda bi, h, g: (0, 0))
    scr = pltpu.VMEM((s, DIL_HD), F32)
    return pl.pallas_call(
        functools.partial(_dil_kernel, seq=s),
        grid=(b, DIL_HPG, DIL_GROUPS),
        in_specs=[part(0), part(1), part(2), tab, tab, gain, gain],
        out_specs=pl.BlockSpec((1, s, DIL_HD), lambda bi, h, g: (bi, 0, h)),
        out_shape=jax.ShapeDtypeStruct((b, s, DIL_HPG * DIL_HD), BF16),
        scratch_shapes=[scr] * 7,
        compiler_params=_cparams("parallel", "parallel", "arbitrary"),
        name="dilated_attention",
    )(dil, dil, dil, cos_h, sin_h, gq.reshape(1, DIL_HD), gk.reshape(1, DIL_HD))


def _sigmoid(x):
    return 0.5 * jnp.tanh(0.5 * x) + 0.5


def _merge_kernel(x_ref, ya_ref, yb_ref, yc_ref, z_ref, gate_ref, wb_ref, wo_ref, o_ref):
    d = x_ref.shape[-1]
    mixed = None
    for n, y_ref in enumerate((ya_ref, yb_ref, yc_ref)):
        z = z_ref[:, BRANCH_WIDTH * n:BRANCH_WIDTH * (n + 1)].astype(F32)
        ys = (y_ref[...].astype(F32) * (z * _sigmoid(z))).astype(BF16)
        branch = _dot(ys, wb_ref[n])
        gate = _sigmoid(gate_ref[:, d * n:d * (n + 1)].astype(F32))
        mixed = gate * branch if mixed is None else mixed + gate * branch
    o_ref[...] = x_ref[...] + _dot(mixed.astype(BF16), wo_ref[...])


def _merge(x2, ya, yb, yc, z, gates, w_branch, w_out):
    t, d = x2.shape
    tm = min(t, 512)
    row = lambda w: pl.BlockSpec((tm, w), lambda i: (i, 0))
    return pl.pallas_call(
        _merge_kernel,
        grid=(t // tm,),
        in_specs=[row(d), row(BRANCH_WIDTH), row(BRANCH_WIDTH), row(BRANCH_WIDTH), row(W_Z), row(N_BRANCHES * d),
                  pl.BlockSpec(w_branch.shape, lambda i: (0, 0, 0)), pl.BlockSpec(w_out.shape, lambda i: (0, 0))],
        out_specs=row(d),
        out_shape=jax.ShapeDtypeStruct((t, d), F32),
        compiler_params=_cparams("parallel"),
        name="merge_out",
    )(x2, ya, yb, yc, z, gates, w_branch, w_out)


def _prep_w_in(w_in):
    d = w_in.shape[-2]
    lead = w_in.shape[:-1]
    o_q = 0
    o_kv = o_q + MLA_Q_RANK
    o_za = o_kv + MLA_KV_RANK + MLA_ROPE
    o_dn = o_za + BRANCH_WIDTH
    o_a = o_dn + 3 * DN_WIDTH
    o_zb = o_a + 2 * DN_HEADS
    o_dil = o_zb + BRANCH_WIDTH
    o_zc = o_dil + 3 * DIL_QKV_WIDTH
    o_g = o_zc + BRANCH_WIDTH
    end = o_g + N_BRANCHES * d
    assert end == w_in.shape[-1]
    half = MLA_ROPE // 2
    o_pe = o_kv + MLA_KV_RANK
    pieces = [
        (o_q, MLA_Q_RANK),
        (o_kv, MLA_KV_RANK), (o_pe, half), (None, LANE // 2 - half), (o_pe + half, half), (None, LANE // 2 - half),
        (o_dn, 3 * DN_WIDTH),
        (o_a, 2 * DN_HEADS), (None, W_DNAB - 2 * DN_HEADS),
        (o_dil, 3 * DIL_QKV_WIDTH),
        (o_za, BRANCH_WIDTH), (o_zb, BRANCH_WIDTH), (o_zc, BRANCH_WIDTH),
        (o_g, N_BRANCHES * d),
    ]
    out_w = sum(w for _, w in pieces)
    rows = 128

    atoms = []
    for src, width in pieces:
        step = _INPROJ_COLS if width % LANE == 0 else width
        atoms += [(None if src is None else src + c0, min(step, width - c0)) for c0 in range(0, width, step)]

    def relayout_kernel(w_ref, o_ref):
        group, gw, dst = [], 0, 0
        for src, width in atoms:
            group.append(jnp.zeros((rows, width), F32) if src is None else w_ref[0, :, src:src + width])
            gw += width
            if gw % LANE == 0:
                val = group[0] if len(group) == 1 else jnp.concatenate(group, axis=1)
                o_ref[0, :, dst:dst + gw] = val.astype(BF16)
                dst += gw
                group, gw = [], 0
        assert dst == out_w and not group

    nl = w_in.shape[0]
    return pl.pallas_call(
        relayout_kernel,
        grid=(nl, d // rows),
        in_specs=[pl.BlockSpec((1, rows, end), lambda l, i: (l, i, 0))],
        out_specs=pl.BlockSpec((1, rows, out_w), lambda l, i: (l, i, 0)),
        out_shape=jax.ShapeDtypeStruct((nl, d, out_w), BF16),
        compiler_params=_cparams("parallel", "parallel"),
        name="w_in_relayout",
    )(w_in)


def _prep_w_q_b(w):
    lead = w.shape[:-1]
    w = w.reshape(lead + (MLA_HEADS, MLA_QK))
    w = jnp.concatenate([w[..., :MLA_NOPE], _spread_rope(w[..., MLA_NOPE:])], axis=-1)
    return w.reshape(lead + (MLA_HEADS * MLA_HEAD_PAD,)).astype(BF16)


def _prep_w_kv_b(w):
    lead = w.shape[:-1]
    w = w.reshape(lead + (MLA_HEADS, MLA_NOPE + MLA_V))
    k = w[..., :MLA_NOPE].reshape(lead + (MLA_HEADS * MLA_NOPE,))
    v = w[..., MLA_NOPE:].reshape(lead + (MLA_HEADS * MLA_V,))
    return k.astype(BF16), jnp.swapaxes(v, -1, -2).astype(BF16)


def _prep_qk_gain(g):
    return jnp.stack([g[..., :MLA_NOPE], _spread_rope(g[..., MLA_NOPE:])], axis=-2)


def kernel(x, positions, norm_g, w_in, mla_q_a_norm_g, mla_w_q_b, mla_kv_a_norm_g, mla_w_kv_b, mla_q_norm_g,
           mla_k_norm_g, dn_conv_w, dn_a_log, dn_dt_bias, dn_out_norm_g, dil_q_norm_g, dil_k_norm_g, w_branch,
           w_out):
    b, s, d = x.shape
    t = b * s
    depth = w_in.shape[0]

    cos_h, sin_h, cr, sr = _rope_tables(positions)
    cos_h3 = cos_h.reshape(b, s, LANE)
    sin_h3 = sin_h.reshape(b, s, LANE)

    w_in_p = _prep_w_in(w_in)
    w_q_p = _prep_w_q_b(mla_w_q_b)
    w_k_p, w_vt_p = _prep_w_kv_b(mla_w_kv_b)
    tk = min(s, MLA_KEY_BLOCK)
    gq_p = _prep_qk_gain(mla_q_norm_g)
    gk_p = _prep_qk_gain(mla_k_norm_g)
    w_branch_b = w_branch.astype(BF16)
    w_out_b = w_out.astype(BF16)

    x2 = x.reshape(t, d)
    for l in range(depth):
        qlat, kvlat, dnqkv, dnab, dil, z, gates = _in_projection(x2, norm_g[l], w_in_p[l])
        q, k, vt = _mla_prep(qlat, kvlat, cr, sr, mla_q_a_norm_g[l].reshape(1, -1), w_q_p[l],
                             mla_kv_a_norm_g[l].reshape(1, -1), w_k_p[l], w_vt_p[l], gq_p[l], gk_p[l], tk)
        y_a = _mla_attention(q.reshape(b, s, -1), k.reshape(b, s, -1),
                             vt.reshape(b, s // tk, MLA_HEADS * MLA_VT_ROWS, tk), tk)
        y_b = _deltanet(dnqkv.reshape(b, s, -1), dnab.reshape(b, s, -1), dn_conv_w[l], dn_a_log[l], dn_dt_bias[l],
                        dn_out_norm_g[l])
        y_c = _dilated_attention(dil.reshape(b, s, -1), cos_h3, sin_h3, dil_q_norm_g[l], dil_k_norm_g[l])
        x2 = _merge(x2, y_a.reshape(t, -1), y_b.reshape(t, -1), y_c.reshape(t, -1), z, gates, w_branch_b[l],
                    w_out_b[l])
    return x2.reshape(b, s, d)
```

```python
import functools

import jax
import jax.numpy as jnp
from jax import lax
from jax.experimental import pallas as pl
from jax.experimental.pallas import tpu as pltpu

F32 = jnp.float32
BF16 = jnp.bfloat16

RMS_EPS = 1e-6
ROPE_THETA = 10000.0
LANE = 128

MLA_HEADS = 4
MLA_NOPE = 128
MLA_ROPE = 64
MLA_V = 128
MLA_QK = MLA_NOPE + MLA_ROPE
MLA_Q_RANK = 384
MLA_KV_RANK = 256
MLA_KV_PAD = 384
MLA_HEAD_PAD = 256
MLA_VT_ROWS = MLA_V + 16
LOG2_E = 1.4426950408889634
MLA_KEY_BLOCK = 512

DN_HEADS = 4
DN_HD = 128
DN_WIDTH = DN_HEADS * DN_HD
DN_CONV = 4
DN_CHUNK = 64
DN_HALO = 16
DN_TILE = 1024
DN_PREP_UNROLL = 4

DIL_WINDOWS = (128, 512, 2048)
DIL_DILATIONS = (1, 4, 16)
DIL_GROUPS = 3
DIL_HPG = 4
DIL_HD = 128
DIL_BLOCK = 128
DIL_QKV_WIDTH = DIL_GROUPS * DIL_HPG * DIL_HD
DIL_UNROLL = 8
DIL_ORDER = (2, 0, 1)
DIL_RELAYOUT_STRIDE = 4

N_BRANCHES = 3
BRANCH_WIDTH = 512

W_QLAT = MLA_Q_RANK
W_KVLAT = MLA_KV_PAD
W_DNQKV = 3 * DN_WIDTH
W_DNAB = LANE
W_DIL = 3 * DIL_QKV_WIDTH
W_Z = N_BRANCHES * BRANCH_WIDTH

VMEM_LIMIT = 56 * 1024 * 1024


def _cparams(*sem):
    return pltpu.CompilerParams(dimension_semantics=sem, vmem_limit_bytes=VMEM_LIMIT)


def _nt(a, b):
    return lax.dot_general(a, b, (((1,), (1,)), ((), ())), preferred_element_type=F32)


def _dot(a, b):
    return jnp.dot(a, b, preferred_element_type=F32)


def _run_staged(gens, skew):
    results = [None] * len(gens)
    done = [False] * len(gens)
    t = 0
    while not all(done):
        for n, gen in enumerate(gens):
            if n * skew <= t and not done[n]:
                try:
                    next(gen)
                except StopIteration as stop:
                    results[n] = stop.value
                    done[n] = True
        t += 1
    return results


def _dot_split3(a, x):
    hi = x.astype(BF16)
    r1 = x - hi.astype(F32)
    mid = r1.astype(BF16)
    lo = (r1 - mid.astype(F32)).astype(BF16)
    return _dot(a, hi) + _dot(a, mid) + _dot(a, lo)


def _rope_kernel(pos_ref, f_ref, cosh_ref, sinh_ref, cr_ref, sr_ref):
    half_h, half_r = DIL_HD // 2, MLA_ROPE // 2
    ang = pos_ref[...].astype(F32) * f_ref[0:1, :]
    c, s = jnp.cos(ang), jnp.sin(ang)
    c_sw, s_sw = pltpu.roll(c, half_h, 1), pltpu.roll(s, half_h, 1)
    lane = lax.broadcasted_iota(jnp.int32, ang.shape, 1)
    low = lane < half_h
    cosh_ref[...] = jnp.where(low, c, c_sw)
    sinh_ref[...] = jnp.where(low, -s, s_sw)
    first = lane < half_r
    second = jnp.logical_and(lane >= half_h, lane < half_h + half_r)
    cr_ref[...] = jnp.where(first, c_sw, jnp.where(second, c, 0.0))
    sr_ref[...] = jnp.where(first, -s_sw, jnp.where(second, s, 0.0))


def _spread_rope(a):
    half = MLA_ROPE // 2
    z = jnp.zeros(a.shape[:-1] + (LANE // 2 - half,), a.dtype)
    return jnp.concatenate([a[..., :half], z, a[..., half:], z], axis=-1)


def _rope_tables(positions):
    t = positions.size
    ts = min(t, 1024)
    inv_h = 1.0 / (ROPE_THETA ** (jnp.arange(0, DIL_HD, 2, dtype=F32) / DIL_HD))
    inv_r = 1.0 / (ROPE_THETA ** (jnp.arange(0, MLA_ROPE, 2, dtype=F32) / MLA_ROPE))
    freqs = jnp.concatenate([inv_h, inv_r, jnp.zeros(LANE - inv_h.size - inv_r.size, F32)])
    tab = jax.ShapeDtypeStruct((t, LANE), F32)
    spec = pl.BlockSpec((ts, LANE), lambda i: (i, 0))
    return pl.pallas_call(
        _rope_kernel,
        grid=(t // ts,),
        in_specs=[pl.BlockSpec((ts, 1), lambda i: (i, 0)), pl.BlockSpec((8, LANE), lambda i: (0, 0))],
        out_specs=[spec] * 4,
        out_shape=[tab] * 4,
        compiler_params=_cparams("parallel"),
        name="rope_tables",
    )(positions.reshape(t, 1), jnp.broadcast_to(freqs, (8, LANE)))


_INPROJ_COLS = 512


def _inproj_kernel(x_ref, g_ref, w_ref, *out_refs):
    x = x_ref[...]
    h = (x * lax.rsqrt(jnp.mean(x * x, axis=-1, keepdims=True) + RMS_EPS) * g_ref[...]).astype(BF16)
    off = 0
    for o_ref in out_refs:
        width = o_ref.shape[-1]
        for c0 in range(0, width, _INPROJ_COLS):
            cw = min(_INPROJ_COLS, width - c0)
            acc = _dot(h, w_ref[:, off + c0:off + c0 + cw])
            o_ref[:, c0:c0 + cw] = acc.astype(o_ref.dtype)
        off += width


def _in_projection(x2, norm_g, w_in_p):
    t, d = x2.shape
    tm = min(t, 256)
    widths = (W_QLAT, W_KVLAT, W_DNQKV, W_DNAB, W_DIL, W_Z, N_BRANCHES * d)
    dtypes = (BF16, BF16, BF16, F32, BF16, BF16, BF16)
    assert sum(widths) == w_in_p.shape[1]
    return pl.pallas_call(
        _inproj_kernel,
        grid=(t // tm,),
        in_specs=[
            pl.BlockSpec((tm, d), lambda i: (i, 0)),
            pl.BlockSpec((1, d), lambda i: (0, 0)),
            pl.BlockSpec(memory_space=pltpu.VMEM),
        ],
        out_specs=[pl.BlockSpec((tm, w), lambda i: (i, 0)) for w in widths],
        out_shape=[jax.ShapeDtypeStruct((t, w), dt) for w, dt in zip(widths, dtypes)],
        compiler_params=_cparams("parallel"),
        name="in_projection",
    )(x2, norm_g.reshape(1, d), w_in_p)


def _mla_prep_kernel(qlat_ref, kvlat_ref, cr_ref, sr_ref, gqa_ref, wq_ref, gkva_ref, wk_ref, wvt_ref,
                     gq_ref, gk_ref, q_ref, k_ref, vt_ref):
    ones_sq = jnp.ones((LANE, LANE), BF16)

    def rms(x, gain, n):
        sq = x * x
        part = sq[:, :LANE]
        for c0 in range(LANE, x.shape[1], LANE):
            part = part + sq[:, c0:c0 + LANE]
        inv = lax.rsqrt(_dot(part.astype(BF16), ones_sq) * (1.0 / n) + RMS_EPS)
        if x.shape[1] > LANE:
            inv = jnp.concatenate([inv] * (x.shape[1] // LANE), axis=1)
        return x * inv * gain

    cr, sr = cr_ref[...], sr_ref[...]

    def rope(x):
        return x * cr + pltpu.roll(x, LANE // 2, 1) * sr

    scale = MLA_QK ** -0.5 * LOG2_E
    qn = rms(qlat_ref[...].astype(F32), gqa_ref[...], MLA_Q_RANK).astype(BF16)
    q = _dot(qn, wq_ref[...])
    kvl = kvlat_ref[...].astype(F32)
    cn = rms(kvl[:, :MLA_KV_RANK], gkva_ref[...], MLA_KV_RANK).astype(BF16)
    kn = _dot(cn, wk_ref[...])
    vt = _nt(wvt_ref[...], cn).astype(BF16)
    for h in range(MLA_HEADS):
        r0 = MLA_VT_ROWS * h
        vt_ref[0, r0:r0 + MLA_V, :] = vt[MLA_V * h:MLA_V * (h + 1)]
        vt_ref[0, r0 + MLA_V:r0 + MLA_VT_ROWS, :] = jnp.ones((MLA_VT_ROWS - MLA_V, vt.shape[1]), BF16)
    k_pe = rope(rms(kvl[:, MLA_KV_RANK:], gk_ref[1:2, :], MLA_ROPE)).astype(BF16)
    for h in range(MLA_HEADS):
        c0 = MLA_HEAD_PAD * h
        q_nope = rms(q[:, c0:c0 + MLA_NOPE], gq_ref[0:1, :], MLA_NOPE)
        q_pe = rope(rms(q[:, c0 + MLA_NOPE:c0 + MLA_HEAD_PAD], gq_ref[1:2, :], MLA_ROPE))
        q_ref[:, c0:c0 + MLA_NOPE] = (q_nope * scale).astype(BF16)
        q_ref[:, c0 + MLA_NOPE:c0 + MLA_HEAD_PAD] = (q_pe * scale).astype(BF16)
        k_nope = rms(kn[:, MLA_NOPE * h:MLA_NOPE * (h + 1)], gk_ref[0:1, :], MLA_NOPE)
        k_ref[:, c0:c0 + MLA_NOPE] = k_nope.astype(BF16)
        k_ref[:, c0 + MLA_NOPE:c0 + MLA_HEAD_PAD] = k_pe


def _mla_prep(qlat, kvlat, cr, sr, gqa, wq, gkva, wk, wvt, gq, gk, tm):
    t = qlat.shape[0]
    row = lambda w: pl.BlockSpec((tm, w), lambda i: (i, 0))
    full = lambda a: pl.BlockSpec(a.shape, lambda i: (0,) * a.ndim)
    qk_w = MLA_HEADS * MLA_HEAD_PAD
    v_w = MLA_HEADS * MLA_VT_ROWS
    return pl.pallas_call(
        _mla_prep_kernel,
        grid=(t // tm,),
        in_specs=[row(W_QLAT), row(W_KVLAT), row(LANE), row(LANE),
                  full(gqa), full(wq), full(gkva), full(wk), full(wvt), full(gq), full(gk)],
        out_specs=[row(qk_w), row(qk_w), pl.BlockSpec((1, v_w, tm), lambda i: (i, 0, 0))],
        out_shape=[jax.ShapeDtypeStruct((t, qk_w), BF16), jax.ShapeDtypeStruct((t, qk_w), BF16),
                   jax.ShapeDtypeStruct((t // tm, v_w, tm), BF16)],
        compiler_params=_cparams("parallel"),
        name="mla_prep",
    )(qlat, kvlat, cr, sr, gqa, wq, gkva, wk, wvt, gq, gk)


def _mla_attn_kernel(q_ref, k_ref, vt_ref, o_ref, m_ref, acc_ref, *, tk):
    qi = pl.program_id(1)
    m_ref[...] = jnp.full(m_ref.shape, -jnp.inf, F32)
    acc_ref[...] = jnp.zeros(acc_ref.shape, F32)

    def head_step(j, h, diagonal):
        start = pl.multiple_of(j * tk, tk)
        cols = slice(MLA_HEAD_PAD * h, MLA_HEAD_PAD * (h + 1))
        st = _nt(k_ref[0, pl.ds(start, tk), cols], q_ref[0, :, cols])
        yield
        if diagonal:
            key = lax.broadcasted_iota(jnp.int32, st.shape, 0)
            qry = lax.broadcasted_iota(jnp.int32, st.shape, 1)
            st = jnp.where(key <= qry, st, -jnp.inf)
        m_old = m_ref[h]
        m_new = jnp.maximum(m_old, jnp.max(st, axis=0, keepdims=True))
        yield
        alpha = jnp.exp2(m_old - m_new)
        p = jnp.exp2(st - m_new)
        yield
        p = p.astype(BF16)
        acc_old = alpha * acc_ref[h]
        yield
        vt = vt_ref[0, j, MLA_VT_ROWS * h:MLA_VT_ROWS * (h + 1), :]
        acc_ref[h] = acc_old + _dot(vt, p)
        m_ref[h] = m_new

    def step(j, diagonal):
        _run_staged([head_step(j, h, diagonal) for h in range(MLA_HEADS)], skew=1)

    def body(j, carry):
        step(j, False)
        return carry

    lax.fori_loop(0, qi, body, 0)
    step(qi, True)

    for h in range(MLA_HEADS):
        o = acc_ref[h, :MLA_V, :] / acc_ref[h, MLA_V:MLA_V + 1, :]
        o_ref[0, :, MLA_V * h:MLA_V * (h + 1)] = o.T.astype(o_ref.dtype)


def _mla_attention(q, k, vt, tk):
    b, s, qk_w = q.shape
    v_w = MLA_HEADS * MLA_V
    vt_w = MLA_HEADS * MLA_VT_ROWS
    tq = tk
    return pl.pallas_call(
        functools.partial(_mla_attn_kernel, tk=tk),
        grid=(b, s // tq),
        in_specs=[
            pl.BlockSpec((1, tq, qk_w), lambda bi, i: (bi, i, 0)),
            pl.BlockSpec((1, s, qk_w), lambda bi, i: (bi, 0, 0)),
            pl.BlockSpec((1, s // tk, vt_w, tk), lambda bi, i: (bi, 0, 0, 0)),
        ],
        out_specs=pl.BlockSpec((1, tq, v_w), lambda bi, i: (bi, i, 0)),
        out_shape=jax.ShapeDtypeStruct((b, s, v_w), BF16),
        scratch_shapes=[pltpu.VMEM((MLA_HEADS, 1, tq), F32), pltpu.VMEM((MLA_HEADS, MLA_VT_ROWS, tq), F32)],
        compiler_params=_cparams("parallel", "arbitrary"),
        name="mla_attention",
    )(q, k, vt)


def _block_diag(x, nblk, bw):
    blk = lax.broadcasted_iota(jnp.int32, x.shape, 1) // bw
    return jnp.concatenate([jnp.where(blk == h, x, 0.0) for h in range(nblk)], axis=0)


def _dn_kernel(x_ref, halo_ref, ab_ref, cw_ref, prm_ref, og_ref, y_ref,
               s_ref, xs_ref, q_s, k_s, v_s, g_s, b_s, cg_s, bm_s, hm_s, egl_s, *, tc):
    i = pl.program_id(1)
    nh, hd, ck = DN_HEADS, DN_HD, DN_CHUNK

    @pl.when(i == 0)
    def _():
        s_ref[...] = jnp.zeros(s_ref.shape, F32)

    xs_ref[0:DN_HALO, :] = jnp.where(i > 0, halo_ref[0].astype(F32), 0.0)
    xs_ref[DN_HALO:DN_HALO + tc, :] = x_ref[0].astype(F32)
    neg_a = -jnp.exp(prm_ref[0:1, :])
    lane = lax.broadcasted_iota(jnp.int32, (ck, LANE), 1)

    def mix_chunk(c):
        r0 = c * ck
        for cb in range(3 * nh):
            cols = slice(hd * cb, hd * (cb + 1))
            y = None
            for j in range(DN_CONV):
                t0 = DN_HALO + r0 - (DN_CONV - 1) + j
                tap = xs_ref[t0:t0 + ck, cols] * cw_ref[j:j + 1, cols]
                y = tap if y is None else y + tap
            y = y * _sigmoid(y)
            if cb < 2 * nh:
                y = y * lax.rsqrt(jnp.sum(y * y, axis=-1, keepdims=True) + 1e-6)
            if cb < nh:
                q_s[r0:r0 + ck, cols] = y * (hd ** -0.5)
            elif cb < 2 * nh:
                k_s[r0:r0 + ck, hd * (cb - nh):hd * (cb - nh + 1)] = y
            else:
                v_s[r0:r0 + ck, hd * (cb - 2 * nh):hd * (cb - 2 * nh + 1)] = y
            yield
        ab = ab_ref[0, r0:r0 + ck, :]
        xg = ab + prm_ref[1:2, :]
        softplus = jnp.maximum(xg, 0.0) + jnp.log1p(jnp.exp(-jnp.abs(xg)))
        g_s[r0:r0 + ck, :] = jnp.where(lane < nh, neg_a * softplus, 0.0)
        b_s[r0:r0 + ck, :] = _sigmoid(ab)

    cat = nh * ck
    ri = lax.broadcasted_iota(jnp.int32, (ck, ck), 0)
    ci = lax.broadcasted_iota(jnp.int32, (ck, ck), 1)
    tri = jnp.where(ri >= ci, 1.0, 0.0).astype(BF16)
    ones = jnp.ones((ck, ck), BF16)
    ii = lax.broadcasted_iota(jnp.int32, (ck, cat), 0)
    lane_c = lax.broadcasted_iota(jnp.int32, (ck, cat), 1)
    jj = lane_c % ck
    blk_c = lane_c // ck
    blk_k = lax.broadcasted_iota(jnp.int32, (ck, nh * hd), 1) // hd
    og = og_ref[...]

    def prep_chunk(c):
        rows = slice(c * ck, (c + 1) * ck)
        gc = _dot_split3(tri, g_s[rows, :])
        yield
        glast = gc[ck - 1:ck, :]

        def cols_to_heads(a, lane0, width):
            return jnp.concatenate(
                [jnp.broadcast_to(a[:, lane0 + h:lane0 + h + 1], (a.shape[0], width)) for h in range(nh)], axis=1)

        colcat = jnp.zeros((ck, cat), F32)
        for h in range(nh):
            colcat = jnp.where(blk_c == h, jnp.broadcast_to(gc[:, h:h + 1], (ck, cat)), colcat)
        rowcat = _dot_split3(ones, jnp.where(ii == jj, colcat, 0.0))
        yield
        diff = colcat - rowcat
        d_inc = jnp.exp(jnp.where(ii >= jj, diff, -jnp.inf))
        d_str = jnp.where(ii > jj, d_inc, 0.0)

        kc = k_s[rows, :]
        lhs = jnp.concatenate([kc * cols_to_heads(b_s[rows, :], nh, hd), q_s[rows, :]], axis=0).astype(BF16)
        bdk = jnp.concatenate([jnp.where(blk_k == h, kc, 0.0) for h in range(nh)], axis=0).astype(BF16)
        aq = _nt(lhs, bdk)
        yield
        low = aq[:ck] * d_str
        qk = aq[ck:] * d_inc

        m = -low
        p = low
        bdp = _block_diag(p, nh, ck).astype(BF16)
        n_sq = ck.bit_length() - 2
        for _ in range(n_sq):
            p = _dot(p.astype(BF16), bdp)
            yield
            bdp = _block_diag(p, nh, ck).astype(BF16)
            m = m + p + _dot(m.astype(BF16), bdp)
        yield

        kc = k_s[rows, :]
        gc_w = cols_to_heads(gc, 0, hd)
        beta_w = cols_to_heads(b_s[rows, :], nh, hd)
        eg = jnp.exp(gc_w)
        vb = v_s[rows, :] * beta_w
        qg = q_s[rows, :] * eg
        kbg = kc * beta_w * eg
        kd = kc * jnp.exp(cols_to_heads(glast, 0, hd) - gc_w)
        rhs = jnp.concatenate(
            [jnp.concatenate([vb[:, hd * h:hd * (h + 1)], kbg[:, hd * h:hd * (h + 1)]], axis=1) for h in range(nh)],
            axis=0)
        sol = rhs + _dot(_block_diag(m, nh, ck).astype(BF16), rhs.astype(BF16))
        yield
        sol_b = sol.astype(BF16)
        qkuw = _dot(_block_diag(qk, nh, ck).astype(BF16), sol_b)
        cg, bmat = [], []
        for h in range(nh):
            hs = slice(ck * h, ck * (h + 1))
            bc = lax.dot_general(kd[:, hd * h:hd * (h + 1)].astype(BF16), sol_b[hs], (((0,), (0,)), ((), ())),
                                 preferred_element_type=F32)
            bmat.append(bc[:, :hd])
            gmat = qg[:, hd * h:hd * (h + 1)] - qkuw[hs, hd:]
            cg.append(jnp.concatenate([bc[:, hd:], gmat], axis=0).astype(BF16))
        return cg, bmat, qkuw[:, :hd], jnp.broadcast_to(jnp.exp(glast), (8, LANE))

    def recur_chunks(chunks):
        for c in chunks:
            rows = slice(c * ck, (c + 1) * ck)
            egl = egl_s[c]
            rs = [_dot(cg_s[c, h], s_ref[h].astype(BF16)) for h in range(nh)]
            yield
            for h in range(nh):
                s_ref[h] = s_ref[h] * egl[0:1, h:h + 1] - rs[h][:hd] + bm_s[c, h]
                o = rs[h][hd:] + hm_s[c, ck * h:ck * (h + 1), :]
                o = o * lax.rsqrt(jnp.mean(o * o, axis=-1, keepdims=True) + RMS_EPS) * og
                y_ref[0, rows, hd * h:hd * (h + 1)] = o.astype(y_ref.dtype)
            yield

    unroll = DN_PREP_UNROLL
    groups = [list(range(g0, g0 + unroll)) for g0 in range(0, tc // ck, unroll)]
    for step in range(len(groups) + 2):
        gens = []
        if step - 1 in range(len(groups)):
            prep_ids = groups[step - 1]
            gens += [prep_chunk(c) for c in prep_ids]
        if step in range(len(groups)):
            gens += [mix_chunk(c) for c in groups[step]]
        if step - 2 in range(len(groups)):
            gens.append(recur_chunks(groups[step - 2]))
        results = _run_staged(gens, skew=0)
        if step - 1 in range(len(groups)):
            for c, (cgm, bmat, hmat, egl) in zip(prep_ids, results):
                for h in range(nh):
                    cg_s[c, h] = cgm[h]
                    bm_s[c, h] = bmat[h]
                hm_s[c] = hmat
                egl_s[c] = egl


def _deltanet(dnqkv, dnab, conv_w, a_log, dt_bias, out_norm_g):
    b, s, _ = dnqkv.shape
    tc = min(s, DN_TILE)
    hb = tc // DN_HALO
    nck = tc // DN_CHUNK
    assert nck % DN_PREP_UNROLL == 0
    prm = jnp.zeros((8, LANE), F32).at[0, :DN_HEADS].set(a_log).at[1, :DN_HEADS].set(dt_bias)
    return pl.pallas_call(
        functools.partial(_dn_kernel, tc=tc),
        grid=(b, s // tc),
        in_specs=[
            pl.BlockSpec((1, tc, W_DNQKV), lambda bi, i: (bi, i, 0)),
            pl.BlockSpec((1, DN_HALO, W_DNQKV), lambda bi, i: (bi, jnp.maximum(i * hb - 1, 0), 0)),
            pl.BlockSpec((1, tc, LANE), lambda bi, i: (bi, i, 0)),
            pl.BlockSpec((DN_CONV, W_DNQKV), lambda bi, i: (0, 0)),
            pl.BlockSpec((8, LANE), lambda bi, i: (0, 0)),
            pl.BlockSpec((1, DN_HD), lambda bi, i: (0, 0)),
        ],
        out_specs=pl.BlockSpec((1, tc, DN_WIDTH), lambda bi, i: (bi, i, 0)),
        out_shape=jax.ShapeDtypeStruct((b, s, DN_WIDTH), BF16),
        scratch_shapes=[
            pltpu.VMEM((DN_HEADS, DN_HD, DN_HD), F32),
            pltpu.VMEM((DN_HALO + tc, W_DNQKV), F32),
            pltpu.VMEM((tc, DN_WIDTH), F32), pltpu.VMEM((tc, DN_WIDTH), F32), pltpu.VMEM((tc, DN_WIDTH), F32),
            pltpu.VMEM((tc, LANE), F32), pltpu.VMEM((tc, LANE), F32),
            pltpu.VMEM((nck, DN_HEADS, DN_HD + DN_CHUNK, DN_HD), BF16),
            pltpu.VMEM((nck, DN_HEADS, DN_HD, DN_HD), F32),
            pltpu.VMEM((nck, DN_HEADS * DN_CHUNK, DN_HD), F32),
            pltpu.VMEM((nck, 8, LANE), F32),
        ],
        compiler_params=_cparams("parallel", "arbitrary"),
        name="gated_deltanet",
    )(dnqkv, dnqkv, dnab, conv_w, prm, out_norm_g.reshape(1, DN_HD))


_DIL_PREP_ROWS = 1024


def _dil_kernel(q_ref, k_ref, v_ref, cos_ref, sin_ref, gq_ref, gk_ref, o_ref,
                qf, kf, vf, m_run, l_run, acc, tmp, *, seq):
    g = pl.program_id(2)
    blk = DIL_BLOCK
    rt = min(seq, _DIL_PREP_ROWS)
    ones_sq = jnp.ones((DIL_HD, DIL_HD), BF16)

    def prep_tile(t):
        def norm_rope(x, gain, cs, sn):
            ms = _dot((x * x).astype(BF16), ones_sq) * (1.0 / DIL_HD)
            x = x * lax.rsqrt(ms + RMS_EPS) * gain
            return x * cs + pltpu.roll(x, DIL_HD // 2, 1) * sn

        out = []
        for part in range(rt // blk):
            rows = pl.ds(pl.multiple_of(t * rt + part * blk, blk), blk)
            cs, sn = cos_ref[0, rows, :], sin_ref[0, rows, :]
            q_new = norm_rope(q_ref[0, rows, :].astype(F32), gq_ref[...], cs, sn) * (DIL_HD ** -0.5 * LOG2_E)
            k_new = norm_rope(k_ref[0, rows, :].astype(F32), gk_ref[...], cs, sn)
            out.append((rows, q_new, k_new, v_ref[0, rows, :].astype(F32)))
            yield
        return out

    def prep_store(result):
        for rows, q_new, k_new, v_new in result:
            qf[rows, :] = q_new
            kf[rows, :] = k_new
            vf[rows, :] = v_new

    def prep_only(t, carry):
        prep_store(_run_staged([prep_tile(t)], skew=0)[0])
        return carry

    qi = lax.broadcasted_iota(jnp.int32, (blk, 2 * blk), 0)
    kj = lax.broadcasted_iota(jnp.int32, (blk, 2 * blk), 1)
    band = jnp.logical_and(kj >= qi, kj <= qi + blk)
    prev_half = kj < blk

    fct = DIL_RELAYOUT_STRIDE
    quarter = seq // fct
    cls_len = seq // (fct * fct)
    rl = min(cls_len, 256)

    def class_major(src):
        for r1 in range(fct):
            for c0 in range(0, quarter, rl):
                tmp[r1 * quarter + c0:r1 * quarter + c0 + rl, :] = src[pl.ds(r1 + c0 * fct, rl, stride=fct), :]
        for r1 in range(fct):
            for r2 in range(fct):
                for c0 in range(0, cls_len, rl):
                    d0 = (r1 * fct + r2) * cls_len + c0
                    src[d0:d0 + rl, :] = tmp[pl.ds(r1 * quarter + r2 + c0 * fct, rl, stride=fct), :]

    def token_major(src):
        for r1 in range(fct):
            for r2 in range(fct):
                for c0 in range(0, cls_len, rl):
                    d0 = (r1 * fct + r2) * cls_len + c0
                    tmp[pl.ds(r1 * quarter + r2 + c0 * fct, rl, stride=fct), :] = src[d0:d0 + rl, :]
        for r1 in range(fct):
            for c0 in range(0, quarter, rl):
                src[pl.ds(r1 + c0 * fct, rl, stride=fct), :] = tmp[r1 * quarter + c0:r1 * quarter + c0 + rl, :]

    def group(dil, first, relayout):
        nb = seq // (dil * blk)
        if relayout:
            assert first and dil == fct * fct
            lax.fori_loop(0, seq // rt, prep_only, 0)
            for ref in (qf, kf, vf):
                class_major(ref)

        def rows_at(start):
            return pl.ds(start, blk) if (dil == 1 or relayout) else pl.ds(start, blk, stride=dil)

        def block(idx):
            r = idx // nb
            n = idx % nb
            if relayout:
                start = pl.multiple_of(idx * blk, blk)
                pstart = pl.multiple_of(jnp.maximum(idx - 1, 0) * blk, blk)
            else:
                start = n * (blk * dil) + r
                pstart = jnp.maximum(start - blk * dil, r)
            rows = rows_at(start)
            prow = rows_at(pstart)
            qb = qf[rows, :].astype(BF16)
            kcat = jnp.concatenate([kf[prow, :], kf[rows, :]], axis=0).astype(BF16)
            vcat = jnp.concatenate([vf[prow, :], vf[rows, :]], axis=0).astype(BF16)
            s = _nt(qb, kcat)
            yield
            no_prev = jnp.where(n > 0, 0.0, -jnp.inf)
            s = jnp.where(band, s + jnp.where(prev_half, no_prev, 0.0), -jnp.inf)
            m_b = jnp.max(s, axis=-1, keepdims=True)
            yield
            p = jnp.exp2(s - m_b)
            l_b = jnp.sum(p, axis=-1, keepdims=True)
            pv = _dot(p.astype(BF16), vcat)
            yield
            if first:
                return rows, jnp.broadcast_to(m_b, (blk, DIL_HD)), jnp.broadcast_to(l_b, (blk, DIL_HD)), pv
            m_old = m_run[rows, :]
            m_new = jnp.maximum(m_old, m_b)
            a_old = jnp.exp2(m_old - m_new)
            a_b = jnp.exp2(m_b - m_new)
            return rows, m_new, l_run[rows, :] * a_old + l_b * a_b, acc[rows, :] * a_old + pv * a_b

        def store_blocks(results):
            for rows, m_new, l_new, acc_new in results:
                m_run[rows, :] = m_new
                l_run[rows, :] = l_new
                acc[rows, :] = acc_new

        if relayout:
            def block_group(ig, carry):
                store_blocks(_run_staged([block(ig * DIL_UNROLL + un) for un in range(DIL_UNROLL)], skew=0))
                return carry

            lax.fori_loop(0, seq // (blk * DIL_UNROLL), block_group, 0)
            for ref in (m_run, l_run, acc):
                token_major(ref)
            return

        bpt = rt // blk
        assert bpt % dil == 0
        ntiles = seq // rt

        def tile_blocks(t):
            return [block((un % dil) * nb + t * (bpt // dil) + un // dil) for un in range(bpt)]

        def tile_step(t, carry):
            results = _run_staged(tile_blocks(t) + [prep_tile(t + 1)], skew=0)
            store_blocks(results[:bpt])
            prep_store(results[bpt])
            return carry

        prep_only(0, 0)
        lax.fori_loop(0, ntiles - 1, tile_step, 0)
        store_blocks(_run_staged(tile_blocks(ntiles - 1), skew=0))

    for j, gi in enumerate(DIL_ORDER):
        dil = DIL_DILATIONS[gi]
        pl.when(g == j)(functools.partial(group, dil, j == 0, dil == DIL_RELAYOUT_STRIDE ** 2))

    @pl.when(g == DIL_GROUPS - 1)
    def _():
        def fin(t, carry):
            rows = pl.ds(pl.multiple_of(t * rt, rt), rt)
            o_ref[0, rows, :] = (acc[rows, :] / l_run[rows, :]).astype(o_ref.dtype)
            return carry

        lax.fori_loop(0, seq // rt, fin, 0)


def _dilated_attention(dil, cos_h, sin_h, gq, gk):
    b, s, _ = dil.shape
    nheads = DIL_GROUPS * DIL_HPG
    assert all(w // d == DIL_BLOCK for w, d in zip(DIL_WINDOWS, DIL_DILATIONS))
    assert s % (max(DIL_DILATIONS) * DIL_BLOCK) == 0
    assert DIL_ORDER == tuple((j + DIL_ORDER[0]) % DIL_GROUPS for j in range(DIL_GROUPS))
    grp = lambda j: (j + DIL_ORDER[0]) % DIL_GROUPS
    part = lambda p: pl.BlockSpec((1, s, DIL_HD), lambda bi, h, j: (bi, 0, p * nheads + grp(j) * DIL_HPG + h))
    tab = pl.BlockSpec((1, s, DIL_HD), lambda bi, h, g: (bi, 0, 0))
    gain = pl.BlockSpec((1, DIL_HD), lambda bi, h, g: (0, 0))
    scr = pltpu.VMEM((s, DIL_HD), F32)
    return pl.pallas_call(
        functools.partial(_dil_kernel, seq=s),
        grid=(b, DIL_HPG, DIL_GROUPS),
        in_specs=[part(0), part(1), part(2), tab, tab, gain, gain],
        out_specs=pl.BlockSpec((1, s, DIL_HD), lambda bi, h, g: (bi, 0, h)),
        out_shape=jax.ShapeDtypeStruct((b, s, DIL_HPG * DIL_HD), BF16),
        scratch_shapes=[scr] * 7,
        compiler_params=_cparams("parallel", "parallel", "arbitrary"),
        name="dilated_attention",
    )(dil, dil, dil, cos_h, sin_h, gq.reshape(1, DIL_HD), gk.reshape(1, DIL_HD))


def _sigmoid(x):
    return 0.5 * jnp.tanh(0.5 * x) + 0.5


def _merge_kernel(x_ref, ya_ref, yb_ref, yc_ref, z_ref, gate_ref, wb_ref, wo_ref, o_ref):
    d = x_ref.shape[-1]
    mixed = None
    for n, y_ref in enumerate((ya_ref, yb_ref, yc_ref)):
        z = z_ref[:, BRANCH_WIDTH * n:BRANCH_WIDTH * (n + 1)].astype(F32)
        ys = (y_ref[...].astype(F32) * (z * _sigmoid(z))).astype(BF16)
        branch = _dot(ys, wb_ref[n])
        gate = _sigmoid(gate_ref[:, d * n:d * (n + 1)].astype(F32))
        mixed = gate * branch if mixed is None else mixed + gate * branch
    o_ref[...] = x_ref[...] + _dot(mixed.astype(BF16), wo_ref[...])


def _merge(x2, ya, yb, yc, z, gates, w_branch, w_out):
    t, d = x2.shape
    tm = min(t, 512)
    row = lambda w: pl.BlockSpec((tm, w), lambda i: (i, 0))
    return pl.pallas_call(
        _merge_kernel,
        grid=(t // tm,),
        in_specs=[row(d), row(BRANCH_WIDTH), row(BRANCH_WIDTH), row(BRANCH_WIDTH), row(W_Z), row(N_BRANCHES * d),
                  pl.BlockSpec(w_branch.shape, lambda i: (0, 0, 0)), pl.BlockSpec(w_out.shape, lambda i: (0, 0))],
        out_specs=row(d),
        out_shape=jax.ShapeDtypeStruct((t, d), F32),
        compiler_params=_cparams("parallel"),
        name="merge_out",
    )(x2, ya, yb, yc, z, gates, w_branch, w_out)


def _prep_w_in(w_in):
    d = w_in.shape[-2]
    lead = w_in.shape[:-1]
    o_q = 0
    o_kv = o_q + MLA_Q_RANK
    o_za = o_kv + MLA_KV_RANK + MLA_ROPE
    o_dn = o_za + BRANCH_WIDTH
    o_a = o_dn + 3 * DN_WIDTH
    o_zb = o_a + 2 * DN_HEADS
    o_dil = o_zb + BRANCH_WIDTH
    o_zc = o_dil + 3 * DIL_QKV_WIDTH
    o_g = o_zc + BRANCH_WIDTH
    end = o_g + N_BRANCHES * d
    assert end == w_in.shape[-1]
    half = MLA_ROPE // 2
    o_pe = o_kv + MLA_KV_RANK
    pieces = [
        (o_q, MLA_Q_RANK),
        (o_kv, MLA_KV_RANK), (o_pe, half), (None, LANE // 2 - half), (o_pe + half, half), (None, LANE // 2 - half),
        (o_dn, 3 * DN_WIDTH),
        (o_a, 2 * DN_HEADS), (None, W_DNAB - 2 * DN_HEADS),
        (o_dil, 3 * DIL_QKV_WIDTH),
        (o_za, BRANCH_WIDTH), (o_zb, BRANCH_WIDTH), (o_zc, BRANCH_WIDTH),
        (o_g, N_BRANCHES * d),
    ]
    out_w = sum(w for _, w in pieces)
    rows = 128

    atoms = []
    for src, width in pieces:
        step = _INPROJ_COLS if width % LANE == 0 else width
        atoms += [(None if src is None else src + c0, min(step, width - c0)) for c0 in range(0, width, step)]

    def relayout_kernel(w_ref, o_ref):
        group, gw, dst = [], 0, 0
        for src, width in atoms:
            group.append(jnp.zeros((rows, width), BF16) if src is None else w_ref[0, :, src:src + width])
            gw += width
            if gw % LANE == 0:
                val = group[0] if len(group) == 1 else jnp.concatenate(group, axis=1)
                o_ref[:, dst:dst + gw] = val
                dst += gw
                group, gw = [], 0
        assert dst == out_w and not group

    def one_layer(layer):
        return pl.pallas_call(
            relayout_kernel,
            grid=(d // rows,),
            in_specs=[pl.BlockSpec((1, rows, end), lambda i: (layer, i, 0))],
            out_specs=pl.BlockSpec((rows, out_w), lambda i: (i, 0)),
            out_shape=jax.ShapeDtypeStruct((d, out_w), BF16),
            compiler_params=_cparams("parallel"),
            name="w_in_relayout",
        )(w_bf)

    w_bf = w_in.astype(BF16)
    return [one_layer(layer) for layer in range(w_in.shape[0])]


def _prep_w_q_b(w):
    lead = w.shape[:-1]
    w = w.reshape(lead + (MLA_HEADS, MLA_QK))
    w = jnp.concatenate([w[..., :MLA_NOPE], _spread_rope(w[..., MLA_NOPE:])], axis=-1)
    return w.reshape(lead + (MLA_HEADS * MLA_HEAD_PAD,)).astype(BF16)


def _prep_w_kv_b(w):
    lead = w.shape[:-1]
    w = w.reshape(lead + (MLA_HEADS, MLA_NOPE + MLA_V))
    k = w[..., :MLA_NOPE].reshape(lead + (MLA_HEADS * MLA_NOPE,))
    v = w[..., MLA_NOPE:].reshape(lead + (MLA_HEADS * MLA_V,))
    return k.astype(BF16), jnp.swapaxes(v, -1, -2).astype(BF16)


def _prep_qk_gain(g):
    return jnp.stack([g[..., :MLA_NOPE], _spread_rope(g[..., MLA_NOPE:])], axis=-2)


def kernel(x, positions, norm_g, w_in, mla_q_a_norm_g, mla_w_q_b, mla_kv_a_norm_g, mla_w_kv_b, mla_q_norm_g,
           mla_k_norm_g, dn_conv_w, dn_a_log, dn_dt_bias, dn_out_norm_g, dil_q_norm_g, dil_k_norm_g, w_branch,
           w_out):
    b, s, d = x.shape
    t = b * s
    depth = w_in.shape[0]

    cos_h, sin_h, cr, sr = _rope_tables(positions)
    cos_h3 = cos_h.reshape(b, s, LANE)
    sin_h3 = sin_h.reshape(b, s, LANE)

    w_in_p = _prep_w_in(w_in)
    w_q_p = _prep_w_q_b(mla_w_q_b)
    w_k_p, w_vt_p = _prep_w_kv_b(mla_w_kv_b)
    tk = min(s, MLA_KEY_BLOCK)
    gq_p = _prep_qk_gain(mla_q_norm_g)
    gk_p = _prep_qk_gain(mla_k_norm_g)
    w_branch_b = w_branch.astype(BF16)
    w_out_b = w_out.astype(BF16)

    x2 = x.reshape(t, d)
    for l in range(depth):
        qlat, kvlat, dnqkv, dnab, dil, z, gates = _in_projection(x2, norm_g[l], w_in_p[l])
        q, k, vt = _mla_prep(qlat, kvlat, cr, sr, mla_q_a_norm_g[l].reshape(1, -1), w_q_p[l],
                             mla_kv_a_norm_g[l].reshape(1, -1), w_k_p[l], w_vt_p[l], gq_p[l], gk_p[l], tk)
        y_a = _mla_attention(q.reshape(b, s, -1), k.reshape(b, s, -1),
                             vt.reshape(b, s // tk, MLA_HEADS * MLA_VT_ROWS, tk), tk)
        y_b = _deltanet(dnqkv.reshape(b, s, -1), dnab.reshape(b, s, -1), dn_conv_w[l], dn_a_log[l], dn_dt_bias[l],
                        dn_out_norm_g[l])
        y_c = _dilated_attention(dil.reshape(b, s, -1), cos_h3, sin_h3, dil_q_norm_g[l], dil_k_norm_g[l])
        x2 = _merge(x2, y_a.reshape(t, -1), y_b.reshape(t, -1), y_c.reshape(t, -1), z, gates, w_branch_b[l],
                    w_out_b[l])
    return x2.reshape(b, s, d)
```

```python
import functools

import jax
import jax.numpy as jnp
from jax import lax
from jax.experimental import pallas as pl
from jax.experimental.pallas import tpu as pltpu

F32 = jnp.float32
BF16 = jnp.bfloat16

RMS_EPS = 1e-6
ROPE_THETA = 10000.0
LANE = 128

MLA_HEADS = 4
MLA_NOPE = 128
MLA_ROPE = 64
MLA_V = 128
MLA_QK = MLA_NOPE + MLA_ROPE
MLA_Q_RANK = 384
MLA_KV_RANK = 256
MLA_KV_PAD = 384
MLA_HEAD_PAD = 256
MLA_VT_ROWS = MLA_V + 16
LOG2_E = 1.4426950408889634
MLA_KEY_BLOCK = 512

DN_HEADS = 4
DN_HD = 128
DN_WIDTH = DN_HEADS * DN_HD
DN_CONV = 4
DN_CHUNK = 64
DN_HALO = 16
DN_TILE = 1024
DN_PREP_UNROLL = 4

DIL_WINDOWS = (128, 512, 2048)
DIL_DILATIONS = (1, 4, 16)
DIL_GROUPS = 3
DIL_HPG = 4
DIL_HD = 128
DIL_BLOCK = 128
DIL_QKV_WIDTH = DIL_GROUPS * DIL_HPG * DIL_HD
DIL_UNROLL = 16
DIL_ORDER = (2, 0, 1)
DIL_RELAYOUT_STRIDE = 4

N_BRANCHES = 3
BRANCH_WIDTH = 512

W_QLAT = MLA_Q_RANK
W_KVLAT = MLA_KV_PAD
W_DNQKV = 3 * DN_WIDTH
W_DNAB = LANE
W_DIL = 3 * DIL_QKV_WIDTH
W_Z = N_BRANCHES * BRANCH_WIDTH

VMEM_LIMIT = 56 * 1024 * 1024


def _cparams(*sem):
    return pltpu.CompilerParams(dimension_semantics=sem, vmem_limit_bytes=VMEM_LIMIT)


def _nt(a, b):
    return lax.dot_general(a, b, (((1,), (1,)), ((), ())), preferred_element_type=F32)


def _dot(a, b):
    return jnp.dot(a, b, preferred_element_type=F32)


def _run_staged(gens, skew):
    results = [None] * len(gens)
    done = [False] * len(gens)
    t = 0
    while not all(done):
        for n, gen in enumerate(gens):
            if n * skew <= t and not done[n]:
                try:
                    next(gen)
                except StopIteration as stop:
                    results[n] = stop.value
                    done[n] = True
        t += 1
    return results


def _dot_split3(a, x):
    hi = x.astype(BF16)
    r1 = x - hi.astype(F32)
    mid = r1.astype(BF16)
    lo = (r1 - mid.astype(F32)).astype(BF16)
    return _dot(a, hi) + _dot(a, mid) + _dot(a, lo)


def _rope_kernel(pos_ref, f_ref, cosh_ref, sinh_ref, cr_ref, sr_ref):
    half_h, half_r = DIL_HD // 2, MLA_ROPE // 2
    ang = pos_ref[...].astype(F32) * f_ref[0:1, :]
    c, s = jnp.cos(ang), jnp.sin(ang)
    c_sw, s_sw = pltpu.roll(c, half_h, 1), pltpu.roll(s, half_h, 1)
    lane = lax.broadcasted_iota(jnp.int32, ang.shape, 1)
    low = lane < half_h
    cosh_ref[...] = jnp.where(low, c, c_sw)
    sinh_ref[...] = jnp.where(low, -s, s_sw)
    first = lane < half_r
    second = jnp.logical_and(lane >= half_h, lane < half_h + half_r)
    cr_ref[...] = jnp.where(first, c_sw, jnp.where(second, c, 0.0))
    sr_ref[...] = jnp.where(first, -s_sw, jnp.where(second, s, 0.0))


def _spread_rope(a):
    half = MLA_ROPE // 2
    z = jnp.zeros(a.shape[:-1] + (LANE // 2 - half,), a.dtype)
    return jnp.concatenate([a[..., :half], z, a[..., half:], z], axis=-1)


def _rope_tables(positions):
    t = positions.size
    ts = min(t, 1024)
    inv_h = 1.0 / (ROPE_THETA ** (jnp.arange(0, DIL_HD, 2, dtype=F32) / DIL_HD))
    inv_r = 1.0 / (ROPE_THETA ** (jnp.arange(0, MLA_ROPE, 2, dtype=F32) / MLA_ROPE))
    freqs = jnp.concatenate([inv_h, inv_r, jnp.zeros(LANE - inv_h.size - inv_r.size, F32)])
    tab = jax.ShapeDtypeStruct((t, LANE), F32)
    spec = pl.BlockSpec((ts, LANE), lambda i: (i, 0))
    return pl.pallas_call(
        _rope_kernel,
        grid=(t // ts,),
        in_specs=[pl.BlockSpec((ts, 1), lambda i: (i, 0)), pl.BlockSpec((8, LANE), lambda i: (0, 0))],
        out_specs=[spec] * 4,
        out_shape=[tab] * 4,
        compiler_params=_cparams("parallel"),
        name="rope_tables",
    )(positions.reshape(t, 1), jnp.broadcast_to(freqs, (8, LANE)))


_INPROJ_COLS = 512


def _inproj_kernel(x_ref, g_ref, w_ref, *out_refs):
    x = x_ref[...]
    h = (x * lax.rsqrt(jnp.mean(x * x, axis=-1, keepdims=True) + RMS_EPS) * g_ref[...]).astype(BF16)
    off = 0
    for o_ref in out_refs:
        width = o_ref.shape[-1]
        for c0 in range(0, width, _INPROJ_COLS):
            cw = min(_INPROJ_COLS, width - c0)
            acc = _dot(h, w_ref[:, off + c0:off + c0 + cw])
            o_ref[:, c0:c0 + cw] = acc.astype(o_ref.dtype)
        off += width


def _in_projection(x2, norm_g, w_in_p):
    t, d = x2.shape
    tm = min(t, 256)
    widths = (W_QLAT, W_KVLAT, W_DNQKV, W_DNAB, W_DIL, W_Z, N_BRANCHES * d)
    dtypes = (BF16, BF16, BF16, F32, BF16, BF16, BF16)
    assert sum(widths) == w_in_p.shape[1]
    return pl.pallas_call(
        _inproj_kernel,
        grid=(t // tm,),
        in_specs=[
            pl.BlockSpec((tm, d), lambda i: (i, 0)),
            pl.BlockSpec((1, d), lambda i: (0, 0)),
            pl.BlockSpec(memory_space=pltpu.VMEM),
        ],
        out_specs=[pl.BlockSpec((tm, w), lambda i: (i, 0)) for w in widths],
        out_shape=[jax.ShapeDtypeStruct((t, w), dt) for w, dt in zip(widths, dtypes)],
        compiler_params=_cparams("parallel"),
        name="in_projection",
    )(x2, norm_g.reshape(1, d), w_in_p)


def _mla_prep_kernel(qlat_ref, kvlat_ref, cr_ref, sr_ref, gqa_ref, wq_ref, gkva_ref, wk_ref, wvt_ref,
                     gq_ref, gk_ref, q_ref, k_ref, vt_ref):
    ones_sq = jnp.ones((LANE, LANE), BF16)

    def rms(x, gain, n):
        sq = x * x
        part = sq[:, :LANE]
        for c0 in range(LANE, x.shape[1], LANE):
            part = part + sq[:, c0:c0 + LANE]
        inv = lax.rsqrt(_dot(part.astype(BF16), ones_sq) * (1.0 / n) + RMS_EPS)
        if x.shape[1] > LANE:
            inv = jnp.concatenate([inv] * (x.shape[1] // LANE), axis=1)
        return x * inv * gain

    cr, sr = cr_ref[...], sr_ref[...]

    def rope(x):
        return x * cr + pltpu.roll(x, LANE // 2, 1) * sr

    scale = MLA_QK ** -0.5 * LOG2_E
    qn = rms(qlat_ref[...].astype(F32), gqa_ref[...], MLA_Q_RANK).astype(BF16)
    q = _dot(qn, wq_ref[...])
    kvl = kvlat_ref[...].astype(F32)
    cn = rms(kvl[:, :MLA_KV_RANK], gkva_ref[...], MLA_KV_RANK).astype(BF16)
    kn = _dot(cn, wk_ref[...])
    vt = _nt(wvt_ref[...], cn).astype(BF16)
    for h in range(MLA_HEADS):
        r0 = MLA_VT_ROWS * h
        vt_ref[0, r0:r0 + MLA_V, :] = vt[MLA_V * h:MLA_V * (h + 1)]
        vt_ref[0, r0 + MLA_V:r0 + MLA_VT_ROWS, :] = jnp.ones((MLA_VT_ROWS - MLA_V, vt.shape[1]), BF16)
    k_pe = rope(rms(kvl[:, MLA_KV_RANK:], gk_ref[1:2, :], MLA_ROPE)).astype(BF16)
    for h in range(MLA_HEADS):
        c0 = MLA_HEAD_PAD * h
        q_nope = rms(q[:, c0:c0 + MLA_NOPE], gq_ref[0:1, :], MLA_NOPE)
        q_pe = rope(rms(q[:, c0 + MLA_NOPE:c0 + MLA_HEAD_PAD], gq_ref[1:2, :], MLA_ROPE))
        q_ref[:, c0:c0 + MLA_NOPE] = (q_nope * scale).astype(BF16)
        q_ref[:, c0 + MLA_NOPE:c0 + MLA_HEAD_PAD] = (q_pe * scale).astype(BF16)
        k_nope = rms(kn[:, MLA_NOPE * h:MLA_NOPE * (h + 1)], gk_ref[0:1, :], MLA_NOPE)
        k_ref[:, c0:c0 + MLA_NOPE] = k_nope.astype(BF16)
        k_ref[:, c0 + MLA_NOPE:c0 + MLA_HEAD_PAD] = k_pe


def _mla_prep(qlat, kvlat, cr, sr, gqa, wq, gkva, wk, wvt, gq, gk, tm):
    t = qlat.shape[0]
    row = lambda w: pl.BlockSpec((tm, w), lambda i: (i, 0))
    full = lambda a: pl.BlockSpec(a.shape, lambda i: (0,) * a.ndim)
    qk_w = MLA_HEADS * MLA_HEAD_PAD
    v_w = MLA_HEADS * MLA_VT_ROWS
    return pl.pallas_call(
        _mla_prep_kernel,
        grid=(t // tm,),
        in_specs=[row(W_QLAT), row(W_KVLAT), row(LANE), row(LANE),
                  full(gqa), full(wq), full(gkva), full(wk), full(wvt), full(gq), full(gk)],
        out_specs=[row(qk_w), row(qk_w), pl.BlockSpec((1, v_w, tm), lambda i: (i, 0, 0))],
        out_shape=[jax.ShapeDtypeStruct((t, qk_w), BF16), jax.ShapeDtypeStruct((t, qk_w), BF16),
                   jax.ShapeDtypeStruct((t // tm, v_w, tm), BF16)],
        compiler_params=_cparams("parallel"),
        name="mla_prep",
    )(qlat, kvlat, cr, sr, gqa, wq, gkva, wk, wvt, gq, gk)


def _mla_attn_kernel(q_ref, k_ref, vt_ref, o_ref, m_ref, acc_ref, *, tk):
    qi = pl.program_id(1)
    m_ref[...] = jnp.full(m_ref.shape, -jnp.inf, F32)
    acc_ref[...] = jnp.zeros(acc_ref.shape, F32)

    def head_step(j, h, diagonal, nth, done):
        start = pl.multiple_of(j * tk, tk)
        cols = slice(MLA_HEAD_PAD * h, MLA_HEAD_PAD * (h + 1))
        st = _nt(k_ref[0, pl.ds(start, tk), cols], q_ref[0, :, cols])
        yield
        if diagonal:
            key = lax.broadcasted_iota(jnp.int32, st.shape, 0)
            qry = lax.broadcasted_iota(jnp.int32, st.shape, 1)
            st = jnp.where(key <= qry, st, -jnp.inf)
        assert done[h] == nth
        m_old = m_ref[h]
        m_new = jnp.maximum(m_old, jnp.max(st, axis=0, keepdims=True))
        yield
        alpha = jnp.exp2(m_old - m_new)
        p = jnp.exp2(st - m_new)
        yield
        p = p.astype(BF16)
        acc_old = alpha * acc_ref[h]
        yield
        vt = vt_ref[0, j, MLA_VT_ROWS * h:MLA_VT_ROWS * (h + 1), :]
        acc_ref[h] = acc_old + _dot(vt, p)
        m_ref[h] = m_new
        done[h] += 1

    def steps(blocks):
        done = [0] * MLA_HEADS
        _run_staged([head_step(j, h, diagonal, nth, done)
                     for nth, (j, diagonal) in enumerate(blocks) for h in range(MLA_HEADS)], skew=1)

    def pair(jp, carry):
        steps([(2 * jp, False), (2 * jp + 1, False)])
        return carry

    lax.fori_loop(0, qi // 2, pair, 0)

    @pl.when(qi % 2 == 1)
    def _():
        steps([(qi - 1, False), (qi, True)])

    @pl.when(qi % 2 == 0)
    def _():
        steps([(qi, True)])

    for h in range(MLA_HEADS):
        o = acc_ref[h, :MLA_V, :] / acc_ref[h, MLA_V:MLA_V + 1, :]
        o_ref[0, :, MLA_V * h:MLA_V * (h + 1)] = o.T.astype(o_ref.dtype)


def _mla_attention(q, k, vt, tk):
    b, s, qk_w = q.shape
    v_w = MLA_HEADS * MLA_V
    vt_w = MLA_HEADS * MLA_VT_ROWS
    tq = tk
    return pl.pallas_call(
        functools.partial(_mla_attn_kernel, tk=tk),
        grid=(b, s // tq),
        in_specs=[
            pl.BlockSpec((1, tq, qk_w), lambda bi, i: (bi, i, 0)),
            pl.BlockSpec((1, s, qk_w), lambda bi, i: (bi, 0, 0)),
            pl.BlockSpec((1, s // tk, vt_w, tk), lambda bi, i: (bi, 0, 0, 0)),
        ],
        out_specs=pl.BlockSpec((1, tq, v_w), lambda bi, i: (bi, i, 0)),
        out_shape=jax.ShapeDtypeStruct((b, s, v_w), BF16),
        scratch_shapes=[pltpu.VMEM((MLA_HEADS, 1, tq), F32), pltpu.VMEM((MLA_HEADS, MLA_VT_ROWS, tq), F32)],
        compiler_params=_cparams("parallel", "arbitrary"),
        name="mla_attention",
    )(q, k, vt)


def _block_diag(x, nblk, bw):
    blk = lax.broadcasted_iota(jnp.int32, x.shape, 1) // bw
    return jnp.concatenate([jnp.where(blk == h, x, 0.0) for h in range(nblk)], axis=0)


def _dn_kernel(x_ref, halo_ref, ab_ref, cw_ref, prm_ref, og_ref, y_ref,
               s_ref, xs_ref, q_s, k_s, v_s, g_s, b_s, cg_s, bm_s, hm_s, egl_s, *, tc):
    i = pl.program_id(1)
    nh, hd, ck = DN_HEADS, DN_HD, DN_CHUNK

    @pl.when(i == 0)
    def _():
        s_ref[...] = jnp.zeros(s_ref.shape, F32)

    xs_ref[0:DN_HALO, :] = jnp.where(i > 0, halo_ref[0].astype(F32), 0.0)
    xs_ref[DN_HALO:DN_HALO + tc, :] = x_ref[0].astype(F32)
    neg_a = -jnp.exp(prm_ref[0:1, :])
    lane = lax.broadcasted_iota(jnp.int32, (ck, LANE), 1)

    def mix_chunk(c):
        r0 = c * ck
        for cb in range(3 * nh):
            cols = slice(hd * cb, hd * (cb + 1))
            y = None
            for j in range(DN_CONV):
                t0 = DN_HALO + r0 - (DN_CONV - 1) + j
                tap = xs_ref[t0:t0 + ck, cols] * cw_ref[j:j + 1, cols]
                y = tap if y is None else y + tap
            y = y * _sigmoid(y)
            if cb < 2 * nh:
                y = y * lax.rsqrt(jnp.sum(y * y, axis=-1, keepdims=True) + 1e-6)
            if cb < nh:
                q_s[r0:r0 + ck, cols] = y * (hd ** -0.5)
            elif cb < 2 * nh:
                k_s[r0:r0 + ck, hd * (cb - nh):hd * (cb - nh + 1)] = y
            else:
                v_s[r0:r0 + ck, hd * (cb - 2 * nh):hd * (cb - 2 * nh + 1)] = y
            yield
        ab = ab_ref[0, r0:r0 + ck, :]
        xg = ab + prm_ref[1:2, :]
        softplus = jnp.maximum(xg, 0.0) + jnp.log1p(jnp.exp(-jnp.abs(xg)))
        g_s[r0:r0 + ck, :] = jnp.where(lane < nh, neg_a * softplus, 0.0)
        b_s[r0:r0 + ck, :] = _sigmoid(ab)

    cat = nh * ck
    ri = lax.broadcasted_iota(jnp.int32, (ck, ck), 0)
    ci = lax.broadcasted_iota(jnp.int32, (ck, ck), 1)
    tri = jnp.where(ri >= ci, 1.0, 0.0).astype(BF16)
    ones = jnp.ones((ck, ck), BF16)
    ii = lax.broadcasted_iota(jnp.int32, (ck, cat), 0)
    lane_c = lax.broadcasted_iota(jnp.int32, (ck, cat), 1)
    jj = lane_c % ck
    blk_c = lane_c // ck
    blk_k = lax.broadcasted_iota(jnp.int32, (ck, nh * hd), 1) // hd
    og = og_ref[...]

    def prep_chunk(c):
        rows = slice(c * ck, (c + 1) * ck)
        gc = _dot_split3(tri, g_s[rows, :])
        yield
        glast = gc[ck - 1:ck, :]

        def cols_to_heads(a, lane0, width):
            return jnp.concatenate(
                [jnp.broadcast_to(a[:, lane0 + h:lane0 + h + 1], (a.shape[0], width)) for h in range(nh)], axis=1)

        colcat = jnp.zeros((ck, cat), F32)
        for h in range(nh):
            colcat = jnp.where(blk_c == h, jnp.broadcast_to(gc[:, h:h + 1], (ck, cat)), colcat)
        rowcat = _dot_split3(ones, jnp.where(ii == jj, colcat, 0.0))
        yield
        diff = colcat - rowcat
        d_inc = jnp.exp(jnp.where(ii >= jj, diff, -jnp.inf))
        d_str = jnp.where(ii > jj, d_inc, 0.0)

        kc = k_s[rows, :]
        lhs = jnp.concatenate([kc * cols_to_heads(b_s[rows, :], nh, hd), q_s[rows, :]], axis=0).astype(BF16)
        bdk = jnp.concatenate([jnp.where(blk_k == h, kc, 0.0) for h in range(nh)], axis=0).astype(BF16)
        aq = _nt(lhs, bdk)
        yield
        low = aq[:ck] * d_str
        qk = aq[ck:] * d_inc

        m = -low
        p = low
        bdp = _block_diag(p, nh, ck).astype(BF16)
        n_sq = ck.bit_length() - 2
        for _ in range(n_sq):
            p = _dot(p.astype(BF16), bdp)
            yield
            bdp = _block_diag(p, nh, ck).astype(BF16)
            m = m + p + _dot(m.astype(BF16), bdp)
        yield

        kc = k_s[rows, :]
        gc_w = cols_to_heads(gc, 0, hd)
        beta_w = cols_to_heads(b_s[rows, :], nh, hd)
        eg = jnp.exp(gc_w)
        vb = v_s[rows, :] * beta_w
        qg = q_s[rows, :] * eg
        kbg = kc * beta_w * eg
        kd = kc * jnp.exp(cols_to_heads(glast, 0, hd) - gc_w)
        rhs = jnp.concatenate(
            [jnp.concatenate([vb[:, hd * h:hd * (h + 1)], kbg[:, hd * h:hd * (h + 1)]], axis=1) for h in range(nh)],
            axis=0)
        sol = rhs + _dot(_block_diag(m, nh, ck).astype(BF16), rhs.astype(BF16))
        yield
        sol_b = sol.astype(BF16)
        qkuw = _dot(_block_diag(qk, nh, ck).astype(BF16), sol_b)
        cg, bmat = [], []
        for h in range(nh):
            hs = slice(ck * h, ck * (h + 1))
            bc = lax.dot_general(kd[:, hd * h:hd * (h + 1)].astype(BF16), sol_b[hs], (((0,), (0,)), ((), ())),
                                 preferred_element_type=F32)
            bmat.append(bc[:, :hd])
            gmat = qg[:, hd * h:hd * (h + 1)] - qkuw[hs, hd:]
            cg.append(jnp.concatenate([bc[:, hd:], gmat], axis=0).astype(BF16))
        return cg, bmat, qkuw[:, :hd], jnp.broadcast_to(jnp.exp(glast), (8, LANE))

    def recur_chunks(chunks):
        for c in chunks:
            rows = slice(c * ck, (c + 1) * ck)
            egl = egl_s[c]
            rs = [_dot(cg_s[c, h], s_ref[h].astype(BF16)) for h in range(nh)]
            yield
            for h in range(nh):
                s_ref[h] = s_ref[h] * egl[0:1, h:h + 1] - rs[h][:hd] + bm_s[c, h]
                o = rs[h][hd:] + hm_s[c, ck * h:ck * (h + 1), :]
                o = o * lax.rsqrt(jnp.mean(o * o, axis=-1, keepdims=True) + RMS_EPS) * og
                y_ref[0, rows, hd * h:hd * (h + 1)] = o.astype(y_ref.dtype)
            yield

    unroll = DN_PREP_UNROLL
    groups = [list(range(g0, g0 + unroll)) for g0 in range(0, tc // ck, unroll)]
    for step in range(len(groups) + 2):
        gens = []
        if step - 1 in range(len(groups)):
            prep_ids = groups[step - 1]
            gens += [prep_chunk(c) for c in prep_ids]
        if step in range(len(groups)):
            gens += [mix_chunk(c) for c in groups[step]]
        if step - 2 in range(len(groups)):
            gens.append(recur_chunks(groups[step - 2]))
        results = _run_staged(gens, skew=0)
        if step - 1 in range(len(groups)):
            for c, (cgm, bmat, hmat, egl) in zip(prep_ids, results):
                for h in range(nh):
                    cg_s[c, h] = cgm[h]
                    bm_s[c, h] = bmat[h]
                hm_s[c] = hmat
                egl_s[c] = egl


def _deltanet(dnqkv, dnab, conv_w, a_log, dt_bias, out_norm_g):
    b, s, _ = dnqkv.shape
    tc = min(s, DN_TILE)
    hb = tc // DN_HALO
    nck = tc // DN_CHUNK
    assert nck % DN_PREP_UNROLL == 0
    prm = jnp.zeros((8, LANE), F32).at[0, :DN_HEADS].set(a_log).at[1, :DN_HEADS].set(dt_bias)
    return pl.pallas_call(
        functools.partial(_dn_kernel, tc=tc),
        grid=(b, s // tc),
        in_specs=[
            pl.BlockSpec((1, tc, W_DNQKV), lambda bi, i: (bi, i, 0)),
            pl.BlockSpec((1, DN_HALO, W_DNQKV), lambda bi, i: (bi, jnp.maximum(i * hb - 1, 0), 0)),
            pl.BlockSpec((1, tc, LANE), lambda bi, i: (bi, i, 0)),
            pl.BlockSpec((DN_CONV, W_DNQKV), lambda bi, i: (0, 0)),
            pl.BlockSpec((8, LANE), lambda bi, i: (0, 0)),
            pl.BlockSpec((1, DN_HD), lambda bi, i: (0, 0)),
        ],
        out_specs=pl.BlockSpec((1, tc, DN_WIDTH), lambda bi, i: (bi, i, 0)),
        out_shape=jax.ShapeDtypeStruct((b, s, DN_WIDTH), BF16),
        scratch_shapes=[
            pltpu.VMEM((DN_HEADS, DN_HD, DN_HD), F32),
            pltpu.VMEM((DN_HALO + tc, W_DNQKV), F32),
            pltpu.VMEM((tc, DN_WIDTH), F32), pltpu.VMEM((tc, DN_WIDTH), F32), pltpu.VMEM((tc, DN_WIDTH), F32),
            pltpu.VMEM((tc, LANE), F32), pltpu.VMEM((tc, LANE), F32),
            pltpu.VMEM((nck, DN_HEADS, DN_HD + DN_CHUNK, DN_HD), BF16),
            pltpu.VMEM((nck, DN_HEADS, DN_HD, DN_HD), F32),
            pltpu.VMEM((nck, DN_HEADS * DN_CHUNK, DN_HD), F32),
            pltpu.VMEM((nck, 8, LANE), F32),
        ],
        compiler_params=_cparams("parallel", "arbitrary"),
        name="gated_deltanet",
    )(dnqkv, dnqkv, dnab, conv_w, prm, out_norm_g.reshape(1, DN_HD))


_DIL_PREP_ROWS = 1024


def _dil_kernel(q_ref, k_ref, v_ref, cos_ref, sin_ref, gq_ref, gk_ref, o_ref,
                qf, kf, vf, m_run, l_run, acc, tmp, *, seq):
    g = pl.program_id(2)
    blk = DIL_BLOCK
    rt = min(seq, _DIL_PREP_ROWS)
    ones_sq = jnp.ones((DIL_HD, DIL_HD), BF16)

    def prep_tile(t):
        def norm_rope(x, gain, cs, sn):
            ms = _dot((x * x).astype(BF16), ones_sq) * (1.0 / DIL_HD)
            x = x * lax.rsqrt(ms + RMS_EPS) * gain
            return x * cs + pltpu.roll(x, DIL_HD // 2, 1) * sn

        out = []
        for part in range(rt // blk):
            rows = pl.ds(pl.multiple_of(t * rt + part * blk, blk), blk)
            cs, sn = cos_ref[0, rows, :], sin_ref[0, rows, :]
            q_new = norm_rope(q_ref[0, rows, :].astype(F32), gq_ref[...], cs, sn) * (DIL_HD ** -0.5 * LOG2_E)
            k_new = norm_rope(k_ref[0, rows, :].astype(F32), gk_ref[...], cs, sn)
            out.append((rows, q_new, k_new, v_ref[0, rows, :].astype(F32)))
            yield
        return out

    def prep_store(result):
        for rows, q_new, k_new, v_new in result:
            qf[rows, :] = q_new
            kf[rows, :] = k_new
            vf[rows, :] = v_new

    def prep_only(t, carry):
        prep_store(_run_staged([prep_tile(t)], skew=0)[0])
        return carry

    qi = lax.broadcasted_iota(jnp.int32, (blk, 2 * blk), 0)
    kj = lax.broadcasted_iota(jnp.int32, (blk, 2 * blk), 1)
    band = jnp.logical_and(kj >= qi, kj <= qi + blk)
    prev_half = kj < blk

    fct = DIL_RELAYOUT_STRIDE
    quarter = seq // fct
    cls_len = seq // (fct * fct)
    rl = min(cls_len, 256)

    def class_major(src):
        for r1 in range(fct):
            for c0 in range(0, quarter, rl):
                tmp[r1 * quarter + c0:r1 * quarter + c0 + rl, :] = src[pl.ds(r1 + c0 * fct, rl, stride=fct), :]
        for r1 in range(fct):
            for r2 in range(fct):
                for c0 in range(0, cls_len, rl):
                    d0 = (r1 * fct + r2) * cls_len + c0
                    src[d0:d0 + rl, :] = tmp[pl.ds(r1 * quarter + r2 + c0 * fct, rl, stride=fct), :]

    def token_major(src):
        for r1 in range(fct):
            for r2 in range(fct):
                for c0 in range(0, cls_len, rl):
                    d0 = (r1 * fct + r2) * cls_len + c0
                    tmp[pl.ds(r1 * quarter + r2 + c0 * fct, rl, stride=fct), :] = src[d0:d0 + rl, :]
        for r1 in range(fct):
            for c0 in range(0, quarter, rl):
                src[pl.ds(r1 + c0 * fct, rl, stride=fct), :] = tmp[r1 * quarter + c0:r1 * quarter + c0 + rl, :]

    def group(dil, first, relayout):
        nb = seq // (dil * blk)
        if relayout:
            assert first and dil == fct * fct
            lax.fori_loop(0, seq // rt, prep_only, 0)
            for ref in (qf, kf, vf):
                class_major(ref)

        def rows_at(start):
            return pl.ds(start, blk) if (dil == 1 or relayout) else pl.ds(start, blk, stride=dil)

        def block(idx):
            r = idx // nb
            n = idx % nb
            if relayout:
                start = pl.multiple_of(idx * blk, blk)
                pstart = pl.multiple_of(jnp.maximum(idx - 1, 0) * blk, blk)
            else:
                start = n * (blk * dil) + r
                pstart = jnp.maximum(start - blk * dil, r)
            rows = rows_at(start)
            prow = rows_at(pstart)
            qb = qf[rows, :].astype(BF16)
            kcat = jnp.concatenate([kf[prow, :], kf[rows, :]], axis=0).astype(BF16)
            vcat = jnp.concatenate([vf[prow, :], vf[rows, :]], axis=0).astype(BF16)
            s = _nt(qb, kcat)
            yield
            no_prev = jnp.where(n > 0, 0.0, -jnp.inf)
            s = jnp.where(band, s + jnp.where(prev_half, no_prev, 0.0), -jnp.inf)
            m_b = jnp.max(s, axis=-1, keepdims=True)
            yield
            p = jnp.exp2(s - m_b)
            l_b = jnp.sum(p, axis=-1, keepdims=True)
            pv = _dot(p.astype(BF16), vcat)
            yield
            if first:
                return rows, jnp.broadcast_to(m_b, (blk, DIL_HD)), jnp.broadcast_to(l_b, (blk, DIL_HD)), pv
            m_old = m_run[rows, :]
            m_new = jnp.maximum(m_old, m_b)
            a_old = jnp.exp2(m_old - m_new)
            a_b = jnp.exp2(m_b - m_new)
            return rows, m_new, l_run[rows, :] * a_old + l_b * a_b, acc[rows, :] * a_old + pv * a_b

        def store_blocks(results):
            for rows, m_new, l_new, acc_new in results:
                m_run[rows, :] = m_new
                l_run[rows, :] = l_new
                acc[rows, :] = acc_new

        if relayout:
            def block_group(ig, carry):
                store_blocks(_run_staged([block(ig * DIL_UNROLL + un) for un in range(DIL_UNROLL)], skew=0))
                return carry

            lax.fori_loop(0, seq // (blk * DIL_UNROLL), block_group, 0)
            for ref in (m_run, l_run, acc):
                token_major(ref)
            return

        bpt = rt // blk
        assert bpt % dil == 0
        ntiles = seq // rt

        def tile_blocks(t):
            return [block((un % dil) * nb + t * (bpt // dil) + un // dil) for un in range(bpt)]

        def tile_step(t, carry):
            results = _run_staged(tile_blocks(t) + [prep_tile(t + 1)], skew=0)
            store_blocks(results[:bpt])
            prep_store(results[bpt])
            return carry

        prep_only(0, 0)
        lax.fori_loop(0, ntiles - 1, tile_step, 0)
        store_blocks(_run_staged(tile_blocks(ntiles - 1), skew=0))

    for j, gi in enumerate(DIL_ORDER):
        dil = DIL_DILATIONS[gi]
        pl.when(g == j)(functools.partial(group, dil, j == 0, dil == DIL_RELAYOUT_STRIDE ** 2))

    @pl.when(g == DIL_GROUPS - 1)
    def _():
        def fin(t, carry):
            rows = pl.ds(pl.multiple_of(t * rt, rt), rt)
            o_ref[0, rows, :] = (acc[rows, :] / l_run[rows, :]).astype(o_ref.dtype)
            return carry

        lax.fori_loop(0, seq // rt, fin, 0)


def _dilated_attention(dil, cos_h, sin_h, gq, gk):
    b, s, _ = dil.shape
    nheads = DIL_GROUPS * DIL_HPG
    assert all(w // d == DIL_BLOCK for w, d in zip(DIL_WINDOWS, DIL_DILATIONS))
    assert s % (max(DIL_DILATIONS) * DIL_BLOCK) == 0
    assert DIL_ORDER == tuple((j + DIL_ORDER[0]) % DIL_GROUPS for j in range(DIL_GROUPS))
    grp = lambda j: (j + DIL_ORDER[0]) % DIL_GROUPS
    part = lambda p: pl.BlockSpec((1, s, DIL_HD), lambda bi, h, j: (bi, 0, p * nheads + grp(j) * DIL_HPG + h))
    tab = pl.BlockSpec((1, s, DIL_HD), lambda bi, h, g: (bi, 0, 0))
    gain = pl.BlockSpec((1, DIL_HD), lambda bi, h, g: (0, 0))
    scr = pltpu.VMEM((s, DIL_HD), F32)
    return pl.pallas_call(
        functools.partial(_dil_kernel, seq=s),
        grid=(b, DIL_HPG, DIL_GROUPS),
        in_specs=[part(0), part(1), part(2), tab, tab, gain, gain],
        out_specs=pl.BlockSpec((1, s, DIL_HD), lambda bi, h, g: (bi, 0, h)),
        out_shape=jax.ShapeDtypeStruct((b, s, DIL_HPG * DIL_HD), BF16),
        scratch_shapes=[scr] * 7,
        compiler_params=_cparams("parallel", "parallel", "arbitrary"),
        name="dilated_attention",
    )(dil, dil, dil, cos_h, sin_h, gq.reshape(1, DIL_HD), gk.reshape(1, DIL_HD))


def _sigmoid(x):
    return 0.5 * jnp.tanh(0.5 * x) + 0.5


def _merge_kernel(x_ref, ya_ref, yb_ref, yc_ref, z_ref, gate_ref, wb_ref, wo_ref, o_ref):
    d = x_ref.shape[-1]
    mixed = None
    for n, y_ref in enumerate((ya_ref, yb_ref, yc_ref)):
        z = z_ref[:, BRANCH_WIDTH * n:BRANCH_WIDTH * (n + 1)].astype(F32)
        ys = (y_ref[...].astype(F32) * (z * _sigmoid(z))).astype(BF16)
        branch = _dot(ys, wb_ref[n])
        gate = _sigmoid(gate_ref[:, d * n:d * (n + 1)].astype(F32))
        mixed = gate * branch if mixed is None else mixed + gate * branch
    o_ref[...] = x_ref[...] + _dot(mixed.astype(BF16), wo_ref[...])


def _merge(x2, ya, yb, yc, z, gates, w_branch, w_out):
    t, d = x2.shape
    tm = min(t, 512)
    row = lambda w: pl.BlockSpec((tm, w), lambda i: (i, 0))
    return pl.pallas_call(
        _merge_kernel,
        grid=(t // tm,),
        in_specs=[row(d), row(BRANCH_WIDTH), row(BRANCH_WIDTH), row(BRANCH_WIDTH), row(W_Z), row(N_BRANCHES * d),
                  pl.BlockSpec(w_branch.shape, lambda i: (0, 0, 0)), pl.BlockSpec(w_out.shape, lambda i: (0, 0))],
        out_specs=row(d),
        out_shape=jax.ShapeDtypeStruct((t, d), F32),
        compiler_params=_cparams("parallel"),
        name="merge_out",
    )(x2, ya, yb, yc, z, gates, w_branch, w_out)


def _prep_w_in(w_in):
    d = w_in.shape[-2]
    lead = w_in.shape[:-1]
    o_q = 0
    o_kv = o_q + MLA_Q_RANK
    o_za = o_kv + MLA_KV_RANK + MLA_ROPE
    o_dn = o_za + BRANCH_WIDTH
    o_a = o_dn + 3 * DN_WIDTH
    o_zb = o_a + 2 * DN_HEADS
    o_dil = o_zb + BRANCH_WIDTH
    o_zc = o_dil + 3 * DIL_QKV_WIDTH
    o_g = o_zc + BRANCH_WIDTH
    end = o_g + N_BRANCHES * d
    assert end == w_in.shape[-1]
    half = MLA_ROPE // 2
    o_pe = o_kv + MLA_KV_RANK
    pieces = [
        (o_q, MLA_Q_RANK),
        (o_kv, MLA_KV_RANK), (o_pe, half), (None, LANE // 2 - half), (o_pe + half, half), (None, LANE // 2 - half),
        (o_dn, 3 * DN_WIDTH),
        (o_a, 2 * DN_HEADS), (None, W_DNAB - 2 * DN_HEADS),
        (o_dil, 3 * DIL_QKV_WIDTH),
        (o_za, BRANCH_WIDTH), (o_zb, BRANCH_WIDTH), (o_zc, BRANCH_WIDTH),
        (o_g, N_BRANCHES * d),
    ]
    out_w = sum(w for _, w in pieces)
    rows = 128

    atoms = []
    for src, width in pieces:
        step = _INPROJ_COLS if width % LANE == 0 else width
        atoms += [(None if src is None else src + c0, min(step, width - c0)) for c0 in range(0, width, step)]

    def relayout_kernel(w_ref, o_ref):
        group, gw, dst = [], 0, 0
        for src, width in atoms:
            group.append(jnp.zeros((rows, width), BF16) if src is None else w_ref[0, :, src:src + width])
            gw += width
            if gw % LANE == 0:
                val = group[0] if len(group) == 1 else jnp.concatenate(group, axis=1)
                o_ref[:, dst:dst + gw] = val
                dst += gw
                group, gw = [], 0
        assert dst == out_w and not group

    def one_layer(layer):
        return pl.pallas_call(
            relayout_kernel,
            grid=(d // rows,),
            in_specs=[pl.BlockSpec((1, rows, end), lambda i: (layer, i, 0))],
            out_specs=pl.BlockSpec((rows, out_w), lambda i: (i, 0)),
            out_shape=jax.ShapeDtypeStruct((d, out_w), BF16),
            compiler_params=_cparams("parallel"),
            name="w_in_relayout",
        )(w_bf)

    w_bf = w_in.astype(BF16)
    return [one_layer(layer) for layer in range(w_in.shape[0])]


def _prep_w_q_b(w):
    lead = w.shape[:-1]
    w = w.reshape(lead + (MLA_HEADS, MLA_QK))
    w = jnp.concatenate([w[..., :MLA_NOPE], _spread_rope(w[..., MLA_NOPE:])], axis=-1)
    return w.reshape(lead + (MLA_HEADS * MLA_HEAD_PAD,)).astype(BF16)


def _prep_w_kv_b(w):
    lead = w.shape[:-1]
    w = w.reshape(lead + (MLA_HEADS, MLA_NOPE + MLA_V))
    k = w[..., :MLA_NOPE].reshape(lead + (MLA_HEADS * MLA_NOPE,))
    v = w[..., MLA_NOPE:].reshape(lead + (MLA_HEADS * MLA_V,))
    return k.astype(BF16), jnp.swapaxes(v, -1, -2).astype(BF16)


def _prep_qk_gain(g):
    return jnp.stack([g[..., :MLA_NOPE], _spread_rope(g[..., MLA_NOPE:])], axis=-2)


def kernel(x, positions, norm_g, w_in, mla_q_a_norm_g, mla_w_q_b, mla_kv_a_norm_g, mla_w_kv_b, mla_q_norm_g,
           mla_k_norm_g, dn_conv_w, dn_a_log, dn_dt_bias, dn_out_norm_g, dil_q_norm_g, dil_k_norm_g, w_branch,
           w_out):
    b, s, d = x.shape
    t = b * s
    depth = w_in.shape[0]

    cos_h, sin_h, cr, sr = _rope_tables(positions)
    cos_h3 = cos_h.reshape(b, s, LANE)
    sin_h3 = sin_h.reshape(b, s, LANE)

    w_in_p = _prep_w_in(w_in)
    w_q_p = _prep_w_q_b(mla_w_q_b)
    w_k_p, w_vt_p = _prep_w_kv_b(mla_w_kv_b)
    tk = min(s, MLA_KEY_BLOCK)
    gq_p = _prep_qk_gain(mla_q_norm_g)
    gk_p = _prep_qk_gain(mla_k_norm_g)
    w_branch_b = w_branch.astype(BF16)
    w_out_b = w_out.astype(BF16)

    x2 = x.reshape(t, d)
    for l in range(depth):
        qlat, kvlat, dnqkv, dnab, dil, z, gates = _in_projection(x2, norm_g[l], w_in_p[l])
        q, k, vt = _mla_prep(qlat, kvlat, cr, sr, mla_q_a_norm_g[l].reshape(1, -1), w_q_p[l],
                             mla_kv_a_norm_g[l].reshape(1, -1), w_k_p[l], w_vt_p[l], gq_p[l], gk_p[l], tk)
        y_a = _mla_attention(q.reshape(b, s, -1), k.reshape(b, s, -1),
                             vt.reshape(b, s // tk, MLA_HEADS * MLA_VT_ROWS, tk), tk)
        y_b = _deltanet(dnqkv.reshape(b, s, -1), dnab.reshape(b, s, -1), dn_conv_w[l], dn_a_log[l], dn_dt_bias[l],
                        dn_out_norm_g[l])
        y_c = _dilated_attention(dil.reshape(b, s, -1), cos_h3, sin_h3, dil_q_norm_g[l], dil_k_norm_g[l])
        x2 = _merge(x2, y_a.reshape(t, -1), y_b.reshape(t, -1), y_c.reshape(t, -1), z, gates, w_branch_b[l],
                    w_out_b[l])
    return x2.reshape(b, s, d)
```
